```python
import jax
import jax.numpy as jnp
from jax import lax

D_MODEL = 1024
BATCH = 4
SEQ = 4096
DEPTH = 1
DEC_BATCH = 128
DEC_SEQ = 4
PAST_LEN = 16384
PAGE_SIZE = 128

RMS_EPS = 1e-6
ROPE_THETA = 10000.0
NEG_INF = -1e30
Q_BLOCK = 128

MLA_HEADS = 8
MLA_NOPE = 64
MLA_ROPE = 32
MLA_V = 64
MLA_Q_RANK = 256
MLA_KV_RANK = 128
MLA_LATENT = MLA_KV_RANK + MLA_ROPE
MLA_SCALE = (MLA_NOPE + MLA_ROPE) ** -0.5

NSA_HEADS = 8
NSA_KV_HEADS = 2
NSA_GROUP = NSA_HEADS // NSA_KV_HEADS
NSA_HEAD_DIM = 64
NSA_SCALE = NSA_HEAD_DIM ** -0.5
CMP_BLOCK = 32
CMP_HIDDEN = 128
SEL_BLOCK = 64
N_SEL = 16
WINDOW = 512
FORCE_SCORE = 1e4

MIX_WIDTH = MLA_HEADS * MLA_V + NSA_HEADS * NSA_HEAD_DIM
NSA_KV_COLS = 2 * NSA_KV_HEADS * NSA_HEAD_DIM
IN_SIZES = (MLA_Q_RANK, MLA_KV_RANK, MLA_ROPE, NSA_HEADS * NSA_HEAD_DIM, NSA_KV_COLS, NSA_KV_COLS, NSA_KV_COLS, 3 * NSA_HEADS)
IN_COLS = sum(IN_SIZES)

N_EXPERTS = 32
TOP_K = 4
D_FF = 1024
SWIGLU_LIMIT = 7.0
SWIGLU_ALPHA = 1.702
MOE_BLOCK = 128

PLE_DIM = 256

kernel_name = 'hymba_mla_nsa_moe_decode_step'


def rmsnorm(x, g):
    xf = x.astype(jnp.float32)
    y = xf * lax.rsqrt(jnp.mean(xf * xf, axis=-1, keepdims=True) + RMS_EPS)
    return (y * g.astype(jnp.float32)).astype(x.dtype)


def rope(x, pos):
    dim = x.shape[-1]
    inv = ROPE_THETA ** (-jnp.arange(0, dim, 2, dtype=jnp.float32) / dim)
    ang = pos.astype(jnp.float32)[:, None] * inv[None, :]
    cos = jnp.cos(ang)[None, :, None, :]
    sin = jnp.sin(ang)[None, :, None, :]
    x1, x2 = jnp.split(x.astype(jnp.float32), 2, axis=-1)
    return jnp.concatenate([x1 * cos - x2 * sin, x1 * sin + x2 * cos], axis=-1).astype(x.dtype)


def masked_softmax(s, mask):
    p = jax.nn.softmax(jnp.where(mask, s.astype(jnp.float32), NEG_INF), axis=-1)
    return jnp.where(mask, p, 0.0)


def mla_project(c_q, c_kv, k_r, pos, g_q, w_uq, g_kv, w_uk):
    q = jnp.einsum('btr,rhe->bthe', rmsnorm(c_q, g_q), w_uq)
    q_lat = jnp.einsum('bthn,chn->bthc', q[..., :MLA_NOPE], w_uk)
    q_rope = rope(q[..., MLA_NOPE:], pos)
    k_rope = rope(k_r[:, :, None, :], pos)[:, :, 0, :]
    rows = jnp.concatenate([rmsnorm(c_kv, g_kv), k_rope], axis=-1)
    return q_lat, q_rope, rows


def mla_attend(q_lat, q_rope, q_pos, rows, k_pos, w_uv):
    q_full = jnp.concatenate([q_lat, q_rope], axis=-1)
    s = jnp.einsum('bthc,blc->bhtl', q_full, rows) * MLA_SCALE
    p = masked_softmax(s, (k_pos[None, :] <= q_pos[:, None])[None, None])
    o = jnp.einsum('bhtl,blc->bthc', p.astype(rows.dtype), rows)[..., :MLA_KV_RANK]
    o = jnp.einsum('bthc,chv->bthv', o, w_uv)
    b, t = o.shape[:2]
    return o.reshape(b, t, MLA_HEADS * MLA_V)


def mla_prompt(q_lat, q_rope, pos, rows, w_uv):
    b, t = q_lat.shape[:2]

    def block(i):
        start = i * Q_BLOCK
        take = lambda a: lax.dynamic_slice_in_dim(a, start, Q_BLOCK, axis=1)
        return mla_attend(take(q_lat), take(q_rope), lax.dynamic_slice_in_dim(pos, start, Q_BLOCK), rows, pos, w_uv)

    out = lax.map(block, jnp.arange(t // Q_BLOCK))
    return out.transpose(1, 0, 2, 3).reshape(b, t, MLA_HEADS * MLA_V)


def mla_sample(q_lat, q_rope, pos, rows_new, cache_mla, page_table, layer, w_uv):
    b, t = q_lat.shape[:2]
    past = page_table.shape[1] * PAGE_SIZE
    rows_past = cache_mla[layer, page_table].reshape(b, past, MLA_LATENT)
    rows = jnp.concatenate([rows_past, rows_new], axis=1)
    return mla_attend(q_lat, q_rope, pos, rows, jnp.arange(past + t, dtype=jnp.int32), w_uv)


def nsa_project(q_raw, cmp_raw, sel_raw, win_raw, gate_raw, pos):
    b, t = q_raw.shape[:2]
    kv_shape = (b, t, 2, NSA_KV_HEADS, NSA_HEAD_DIM)
    q = q_raw.reshape(b, t, NSA_HEADS, NSA_HEAD_DIM)
    rows_c = cmp_raw.reshape(kv_shape)
    sel = sel_raw.reshape(kv_shape)
    win = win_raw.reshape(kv_shape)
    rows_s = jnp.stack([rope(sel[:, :, 0], pos), sel[:, :, 1]], axis=2)
    rows_w = jnp.stack([rope(win[:, :, 0], pos), win[:, :, 1]], axis=2)
    gates = jax.nn.sigmoid(gate_raw.astype(jnp.float32)).astype(q_raw.dtype)
    return q, rows_c, rows_s, rows_w, gates.reshape(b, t, NSA_KV_HEADS, NSA_GROUP, 3)


def group_heads(q):
    b, t = q.shape[:2]
    return q.reshape(b, t, NSA_KV_HEADS, NSA_GROUP, NSA_HEAD_DIM)


def compress(rows, cmp_pe, w_cmp1, w_cmp2):
    b, l = rows.shape[:2]
    nc = l // CMP_BLOCK
    blk = rows.reshape(b, nc, CMP_BLOCK, 2, NSA_KV_HEADS, NSA_HEAD_DIM) + cmp_pe[:, :, None, :]
    flat = blk.transpose(0, 1, 3, 4, 2, 5).reshape(b, nc, 2, NSA_KV_HEADS, CMP_BLOCK * NSA_HEAD_DIM)
    hid = jax.nn.gelu(jnp.einsum('bnskf,sfh->bnskh', flat, w_cmp1))
    return jnp.einsum('bnskh,shd->bnskd', hid, w_cmp2)


def nsa_cmp_branch(qg, q_pos, blocks_c):
    kc, vc = blocks_c[:, :, 0], blocks_c[:, :, 1]
    nc = kc.shape[1]
    s = jnp.einsum('btkgd,bnkd->btkgn', qg, kc) * NSA_SCALE
    blk_end = jnp.arange(nc, dtype=jnp.int32) * CMP_BLOCK + (CMP_BLOCK - 1)
    p = masked_softmax(s, (blk_end[None, :] <= q_pos[:, None])[None, :, None, None, :])
    o = jnp.einsum('btkgn,bnkd->btkgd', p.astype(vc.dtype), vc)
    imp = p.sum(axis=3)
    ratio = SEL_BLOCK // CMP_BLOCK
    p_slc = imp.reshape(imp.shape[:3] + (nc // ratio, ratio)).sum(-1)
    return o, p_slc


def nsa_select(p_slc, q_pos):
    ns = p_slc.shape[-1]
    j = jnp.arange(ns, dtype=jnp.int32)[None, :]
    qb = (q_pos // SEL_BLOCK)[:, None]
    forced = ((j == 0) | (j == qb) | (j == qb - 1))[None, :, None, :]
    causal = (j * SEL_BLOCK <= q_pos[:, None])[None, :, None, :]
    score = jnp.where(causal, jnp.where(forced, FORCE_SCORE, p_slc), NEG_INF)
    _, idx = lax.top_k(score, min(N_SEL, ns))
    return idx


def gathered_kv(g, idx):
    b, t, k, n = idx.shape
    g = g.reshape(b, t, k, n * SEL_BLOCK, 2, NSA_HEAD_DIM)
    kpos = (idx[..., None] * SEL_BLOCK + jnp.arange(SEL_BLOCK, dtype=jnp.int32)).reshape(b, t, k, n * SEL_BLOCK)
    return g[..., 0, :], g[..., 1, :], kpos


def attend_gathered(qg, q_pos, k, v, kpos):
    s = jnp.einsum('btkgd,btkld->btkgl', qg, k) * NSA_SCALE
    p = masked_softmax(s, (kpos <= q_pos[None, :, None, None])[:, :, :, None, :])
    return jnp.einsum('btkgl,btkld->btkgd', p.astype(v.dtype), v)


def attend_window(qg, q_pos, band, kpos):
    k, v = band[:, :, 0], band[:, :, 1]
    s = jnp.einsum('btkgd,blkd->btkgl', qg, k) * NSA_SCALE
    kp, t = kpos[None, :], q_pos[:, None]
    mask = ((kp >= 0) & (kp <= t) & (kp > t - WINDOW))[None, :, None, None, :]
    p = masked_softmax(s, mask)
    return jnp.einsum('btkgl,blkd->btkgd', p.astype(v.dtype), v)


def nsa_combine(o_cmp, o_sel, o_win, gates):
    o = gates[..., 0:1] * o_cmp + gates[..., 1:2] * o_sel + gates[..., 2:3] * o_win
    b, t = o.shape[:2]
    return o.reshape(b, t, NSA_HEADS * NSA_HEAD_DIM)


def nsa_prompt(q, rows_c, rows_s, rows_w, gates, pos, cmp_pe, w_cmp1, w_cmp2):
    b, t = q.shape[:2]
    blocks_c = compress(rows_c, cmp_pe, w_cmp1, w_cmp2)
    qg, qr = group_heads(q), group_heads(rope(q, pos))
    band_rows = jnp.pad(rows_w, ((0, 0), (WINDOW, 0), (0, 0), (0, 0), (0, 0)))
    b_idx = jnp.arange(b)[:, None, None, None, None]
    k_idx = jnp.arange(NSA_KV_HEADS)[None, None, :, None, None]

    def block(i):
        start = i * Q_BLOCK
        take = lambda a: lax.dynamic_slice_in_dim(a, start, Q_BLOCK, axis=1)
        qp = lax.dynamic_slice_in_dim(pos, start, Q_BLOCK)
        qr_blk = take(qr)
        o_cmp, p_slc = nsa_cmp_branch(take(qg), qp, blocks_c)
        idx = nsa_select(p_slc, qp)
        row_idx = idx[..., None] * SEL_BLOCK + jnp.arange(SEL_BLOCK, dtype=jnp.int32)
        k_sel, v_sel, kpos = gathered_kv(rows_s[b_idx, row_idx, :, k_idx], idx)
        o_sel = attend_gathered(qr_blk, qp, k_sel, v_sel, kpos)
        band = lax.dynamic_slice_in_dim(band_rows, start, WINDOW + Q_BLOCK, axis=1)
        o_win = attend_window(qr_blk, qp, band, start - WINDOW + jnp.arange(WINDOW + Q_BLOCK, dtype=jnp.int32))
        return nsa_combine(o_cmp, o_sel, o_win, take(gates))

    out = lax.map(block, jnp.arange(t // Q_BLOCK))
    return out.transpose(1, 0, 2, 3).reshape(b, t, NSA_HEADS * NSA_HEAD_DIM)


def nsa_sample(q, rows_c, rows_s, rows_w, gates, pos, cache_cmp, cache_sel, win_buf, page_table, layer, cmp_pe, w_cmp1, w_cmp2):
    b, t = q.shape[:2]
    n_pages = page_table.shape[1]
    past = n_pages * PAGE_SIZE
    new_len = -(-t // SEL_BLOCK) * SEL_BLOCK
    pad_new = lambda r: jnp.pad(r, ((0, 0), (0, new_len - t), (0, 0), (0, 0), (0, 0)))

    def page_summaries(n):
        return compress(cache_cmp[layer, page_table[:, n]], cmp_pe, w_cmp1, w_cmp2)

    past_c = lax.map(page_summaries, jnp.arange(n_pages))
    past_c = past_c.transpose(1, 0, 2, 3, 4, 5).reshape(b, past // CMP_BLOCK, 2, NSA_KV_HEADS, NSA_HEAD_DIM)
    blocks_c = jnp.concatenate([past_c, compress(pad_new(rows_c), cmp_pe, w_cmp1, w_cmp2)], axis=1)
    qg, qr = group_heads(q), group_heads(rope(q, pos))
    o_cmp, p_slc = nsa_cmp_branch(qg, pos, blocks_c)

    idx = nsa_select(p_slc, pos)
    past_blocks = past // SEL_BLOCK
    per_page = PAGE_SIZE // SEL_BLOCK
    in_past = idx < past_blocks
    idx_past = jnp.minimum(idx, past_blocks - 1)
    b_idx = jnp.arange(b)[:, None, None, None]
    k_idx = jnp.arange(NSA_KV_HEADS)[None, None, :, None, None]
    phys = page_table[b_idx, idx_past // per_page][..., None]
    offs = (idx_past % per_page)[..., None] * SEL_BLOCK + jnp.arange(SEL_BLOCK, dtype=jnp.int32)
    rows_past = cache_sel[layer, phys, offs, :, k_idx]
    offs_new = jnp.clip((idx - past_blocks)[..., None] * SEL_BLOCK + jnp.arange(SEL_BLOCK, dtype=jnp.int32), 0, new_len - 1)
    rows_new = pad_new(rows_s)[b_idx[..., None], offs_new, :, k_idx]
    gathered = jnp.where(in_past[..., None, None, None], rows_past, rows_new)
    k_sel, v_sel, kpos = gathered_kv(gathered, idx)
    o_sel = attend_gathered(qr, pos, k_sel, v_sel, kpos)

    wb = win_buf.shape[1]
    band = jnp.concatenate([win_buf, rows_w], axis=1)
    band_pos = jnp.concatenate([past - wb + jnp.arange(wb, dtype=jnp.int32), pos])
    o_win = attend_window(qr, pos, band, band_pos)
    return nsa_combine(o_cmp, o_sel, o_win, gates), band[:, t:]


def project_groups(h, pos, g_attn, w_in, g_q, w_uq, g_kv, w_uk):
    proj = rmsnorm(h, g_attn) @ w_in
    parts, start = [], 0
    for size in IN_SIZES:
        parts.append(proj[..., start:start + size])
        start += size
    mla = mla_project(parts[0], parts[1], parts[2], pos, g_q, w_uq, g_kv, w_uk)
    nsa = nsa_project(parts[3], parts[4], parts[5], parts[6], parts[7], pos)
    return mla, nsa


def moe(h, w_router, b_router, w_e_in, b_e_in, w_e_out, b_e_out):
    lead = h.shape[:-1]
    xt = h.reshape(-1, D_MODEL)
    n_tok = xt.shape[0]
    logits = (xt @ w_router + b_router).astype(jnp.float32)
    top_logit, top_e = lax.top_k(logits, TOP_K)
    gate_w = jax.nn.softmax(top_logit, axis=-1)
    n_assign = n_tok * TOP_K
    flat_e = top_e.reshape(n_assign)
    order = jnp.argsort(flat_e)
    e_sorted = flat_e[order]
    tok_sorted = (order // TOP_K).astype(jnp.int32)
    counts = jnp.bincount(flat_e, length=N_EXPERTS)
    padded = (counts + MOE_BLOCK - 1) // MOE_BLOCK * MOE_BLOCK
    pad_end = jnp.cumsum(padded)
    pad_start = pad_end - padded
    grp_start = jnp.cumsum(counts) - counts
    dest = pad_start[e_sorted] + jnp.arange(n_assign, dtype=jnp.int32) - grp_start[e_sorted]
    n_blocks = -(-n_assign // MOE_BLOCK) + N_EXPERTS
    row_tok = jnp.full((n_blocks * MOE_BLOCK,), n_tok, jnp.int32).at[dest].set(tok_sorted)
    block_e = jnp.minimum(jnp.searchsorted(pad_end, jnp.arange(n_blocks, dtype=jnp.int32) * MOE_BLOCK, side='right'), N_EXPERTS - 1)
    x_pad = jnp.concatenate([xt, jnp.zeros((1, D_MODEL), xt.dtype)], axis=0)

    def expert_block(args):
        toks, e = args
        hb = x_pad[toks] @ w_e_in[e] + b_e_in[e]
        g = jnp.minimum(hb[:, :D_FF], SWIGLU_LIMIT)
        u = jnp.clip(hb[:, D_FF:], -SWIGLU_LIMIT, SWIGLU_LIMIT)
        act = (u + 1.0) * g * jax.nn.sigmoid(SWIGLU_ALPHA * g)
        return act @ w_e_out[e] + b_e_out[e]

    out_rows = lax.map(expert_block, (row_tok.reshape(n_blocks, MOE_BLOCK), block_e)).reshape(-1, D_MODEL)
    contrib = out_rows[dest] * gate_w.reshape(n_assign)[order][:, None].astype(out_rows.dtype)
    y = jax.ops.segment_sum(contrib, tok_sorted, num_segments=n_tok)
    return y.reshape(lead + (D_MODEL,))


def channel_and_ple(h, p, g_ffn, w_router, b_router, w_e_in, b_e_in, w_e_out, b_e_out, g_ple, w_ple, w_ple_gate):
    h = h + moe(rmsnorm(h, g_ffn), w_router, b_router, w_e_in, b_e_in, w_e_out, b_e_out)
    return h + (p @ w_ple) * jax.nn.sigmoid(rmsnorm(h, g_ple) @ w_ple_gate)


def setup_inputs(seed: int = 0) -> dict:
    key = jax.random.key(seed)
    ks = jax.random.split(key, 40)
    n_pages = PAST_LEN // PAGE_SIZE
    n_pool = (DEC_BATCH * n_pages * 5) // 4
    win_buf = min(WINDOW, PAST_LEN)
    kv = (2, NSA_KV_HEADS, NSA_HEAD_DIM)

    def nrm(i, shape, scale=1.0):
        return jax.random.normal(ks[i], shape, jnp.float32) * scale

    def gain(i, shape):
        return 1.0 + nrm(i, shape, 0.02)

    page_table = jax.random.permutation(ks[8], n_pool)[:DEC_BATCH * n_pages].reshape(DEC_BATCH, n_pages).astype(jnp.int32)
    return {
        'x_prompt': nrm(0, (BATCH, SEQ, D_MODEL)),
        'x_sample': nrm(1, (DEC_BATCH, DEC_SEQ, D_MODEL)),
        'cache_mla': nrm(2, (DEPTH, n_pool, PAGE_SIZE, MLA_LATENT)),
        'cache_nsa_cmp': nrm(3, (DEPTH, n_pool, PAGE_SIZE) + kv),
        'cache_nsa_sel': nrm(4, (DEPTH, n_pool, PAGE_SIZE) + kv),
        'state_nsa_win': nrm(5, (DEPTH, DEC_BATCH, win_buf) + kv),
        'page_table': page_table,
        'p_prompt': nrm(6, (DEPTH, BATCH, SEQ, PLE_DIM)),
        'p_sample': nrm(7, (DEPTH, DEC_BATCH, DEC_SEQ, PLE_DIM)),
        'g_attn': gain(10, (DEPTH, D_MODEL)),
        'w_in': nrm(11, (DEPTH, D_MODEL, IN_COLS), D_MODEL ** -0.5),
        'g_q': gain(12, (DEPTH, MLA_Q_RANK)),
        'w_uq': nrm(13, (DEPTH, MLA_Q_RANK, MLA_HEADS, MLA_NOPE + MLA_ROPE), MLA_Q_RANK ** -0.5),
        'g_kv': gain(14, (DEPTH, MLA_KV_RANK)),
        'w_uk': nrm(15, (DEPTH, MLA_KV_RANK, MLA_HEADS, MLA_NOPE), MLA_KV_RANK ** -0.5),
        'w_uv': nrm(16, (DEPTH, MLA_KV_RANK, MLA_HEADS, MLA_V), MLA_KV_RANK ** -0.5),
        'cmp_pe': nrm(17, (DEPTH, CMP_BLOCK, 2, NSA_HEAD_DIM), 0.1),
        'w_cmp1': nrm(18, (DEPTH, 2, CMP_BLOCK * NSA_HEAD_DIM, CMP_HIDDEN), (CMP_BLOCK * NSA_HEAD_DIM) ** -0.5),
        'w_cmp2': nrm(19, (DEPTH, 2, CMP_HIDDEN, NSA_HEAD_DIM), CMP_HIDDEN ** -0.5),
        'w_o': nrm(20, (DEPTH, MIX_WIDTH, D_MODEL), MIX_WIDTH ** -0.5),
        'g_ffn': gain(21, (DEPTH, D_MODEL)),
        'w_router': nrm(22, (DEPTH, D_MODEL, N_EXPERTS), D_MODEL ** -0.5),
        'b_router': nrm(23, (DEPTH, N_EXPERTS), 0.01),
        'w_e_in': nrm(24, (DEPTH, N_EXPERTS, D_MODEL, 2 * D_FF), D_MODEL ** -0.5),
        'b_e_in': nrm(25, (DEPTH, N_EXPERTS, 2 * D_FF), 0.02),
        'w_e_out': nrm(26, (DEPTH, N_EXPERTS, D_FF, D_MODEL), D_FF ** -0.5),
        'b_e_out': nrm(27, (DEPTH, N_EXPERTS, D_MODEL), 0.02),
        'g_ple': gain(28, (DEPTH, D_MODEL)),
        'w_ple': nrm(29, (DEPTH, PLE_DIM, D_MODEL), PLE_DIM ** -0.5),
        'w_ple_gate': nrm(30, (DEPTH, D_MODEL, D_MODEL), D_MODEL ** -0.5),
        'g_final': gain(31, (D_MODEL,)),
    }


def reference(x_prompt, x_sample, cache_mla, cache_nsa_cmp, cache_nsa_sel, state_nsa_win, page_table, p_prompt, p_sample,
              g_attn, w_in, g_q, w_uq, g_kv, w_uk, w_uv, cmp_pe, w_cmp1, w_cmp2, w_o, g_ffn, w_router, b_router,
              w_e_in, b_e_in, w_e_out, b_e_out, g_ple, w_ple, w_ple_gate, g_final):
    t_p = x_prompt.shape[1]
    t_s = x_sample.shape[1]
    past = page_table.shape[1] * PAGE_SIZE
    pos_p = jnp.arange(t_p, dtype=jnp.int32)
    pos_s = past + jnp.arange(t_s, dtype=jnp.int32)
    hp, hs = x_prompt, x_sample
    mla_p, mla_s, cmp_p, cmp_s, sel_p, sel_s, win_p, win_s = [], [], [], [], [], [], [], []
    for l in range(DEPTH):
        (q_lat, q_rope, rows_a), (q_b, rows_c, rows_s, rows_w, gates) = project_groups(
            hp, pos_p, g_attn[l], w_in[l], g_q[l], w_uq[l], g_kv[l], w_uk[l])
        o_a = mla_prompt(q_lat, q_rope, pos_p, rows_a, w_uv[l])
        o_b = nsa_prompt(q_b, rows_c, rows_s, rows_w, gates, pos_p, cmp_pe[l], w_cmp1[l], w_cmp2[l])
        hp = hp + jnp.concatenate([o_a, o_b], axis=-1) @ w_o[l]
        hp = channel_and_ple(hp, p_prompt[l], g_ffn[l], w_router[l], b_router[l], w_e_in[l], b_e_in[l],
                             w_e_out[l], b_e_out[l], g_ple[l], w_ple[l], w_ple_gate[l])
        mla_p.append(rows_a)
        cmp_p.append(rows_c)
        sel_p.append(rows_s)
        win_p.append(rows_w[:, t_p - min(WINDOW, t_p):])

        (q_lat, q_rope, rows_a), (q_b, rows_c, rows_s, rows_w, gates) = project_groups(
            hs, pos_s, g_attn[l], w_in[l], g_q[l], w_uq[l], g_kv[l], w_uk[l])
        o_a = mla_sample(q_lat, q_rope, pos_s, rows_a, cache_mla, page_table, l, w_uv[l])
        o_b, win_new = nsa_sample(q_b, rows_c, rows_s, rows_w, gates, pos_s, cache_nsa_cmp, cache_nsa_sel,
                                  state_nsa_win[l], page_table, l, cmp_pe[l], w_cmp1[l], w_cmp2[l])
        hs = hs + jnp.concatenate([o_a, o_b], axis=-1) @ w_o[l]
        hs = channel_and_ple(hs, p_sample[l], g_ffn[l], w_router[l], b_router[l], w_e_in[l], b_e_in[l],
                             w_e_out[l], b_e_out[l], g_ple[l], w_ple[l], w_ple_gate[l])
        mla_s.append(rows_a)
        cmp_s.append(rows_c)
        sel_s.append(rows_s)
        win_s.append(win_new)
    return (rmsnorm(hp, g_final), rmsnorm(hs, g_final), jnp.stack(mla_p), jnp.stack(mla_s), jnp.stack(cmp_p), jnp.stack(cmp_s), jnp.stack(sel_p), jnp.stack(sel_s), jnp.stack(win_p), jnp.stack(win_s))
```

```python
import functools

import jax
import jax.numpy as jnp
from jax import lax
from jax.experimental import pallas as pl
from jax.experimental.pallas import tpu as pltpu

RMS_EPS = 1e-6
ROPE_THETA = 10000.0
NEG_INF = -1e30
REMOVED = -3e38
FORCE_SCORE = 1e4
SEL_BLOCK = 64
N_SEL = 16
TOP_K = 4
SWIGLU_LIMIT = 7.0
SWIGLU_ALPHA = 1.702

LANES = 128
VMEM_LIMIT = 56 * 1024 * 1024

F32 = jnp.float32
BF16 = jnp.bfloat16


def _cparams(sem):
    return pltpu.CompilerParams(dimension_semantics=sem, vmem_limit_bytes=VMEM_LIMIT)


def _nt(a, b):
    return lax.dot_general(a, b, (((1,), (1,)), ((), ())), preferred_element_type=F32)


def _nn(a, b):
    return jnp.dot(a, b, preferred_element_type=F32)


def _split(x):
    hi = x.astype(BF16)
    lo = (x - hi.astype(F32)).astype(BF16)
    return hi, lo


def _lane(shape, dim=None):
    return lax.broadcasted_iota(jnp.int32, shape, len(shape) - 1 if dim is None else dim)


def _partner(x, half):
    w = x.shape[-1]
    lane = _lane(x.shape)
    first = (lane % (2 * half)) < half
    return jnp.where(first, pltpu.roll(x, w - half, x.ndim - 1), pltpu.roll(x, half, x.ndim - 1))


def _swap64(x):
    return pltpu.roll(x, 64, x.ndim - 1)


def _proj_kernel(x_ref, ga_ref, win_ref, gq_ref, gkv_ref, wnope_ref, wuk_ref, wr_ref, wrp_ref,
                 cosa_ref, sina_ref, cosb_ref, sinb_ref,
                 rowsa_ref, kmla_ref, qmla_ref, rowsc_ref, rowss_ref, rowsw_ref,
                 kvs_ref, kvw_ref, qraw_ref, qrope_ref, gates_ref,
                 wabs_ref, *, mla_scale, nsa_scale):
    n_heads = qmla_ref.shape[0]

    @pl.when(pl.program_id(0) == 0)
    def _():
        for h in range(n_heads):
            a_hi, a_lo = _split(wnope_ref[h])
            b_hi, b_lo = _split(wuk_ref[h])
            w = _nt(a_hi, b_hi) + _nt(a_lo, b_hi) + _nt(a_hi, b_lo)
            wabs_ref[:, h * LANES:(h + 1) * LANES] = w.astype(BF16)

    x = x_ref[...]
    xn = x * lax.rsqrt(jnp.mean(x * x, axis=-1, keepdims=True) + RMS_EPS) * ga_ref[...]
    proj = _nn(xn.astype(BF16), win_ref[...])
    cq = proj[:, 0:256]
    ckv = proj[:, 256:384]
    qb = proj[:, 384:896]
    cmp_raw = proj[:, 896:1152]
    sel_raw = proj[:, 1152:1408]
    win_raw = proj[:, 1408:1664]
    misc = proj[:, 1664:1792]

    cosa, sina = cosa_ref[...], sina_ref[...]
    cosb, sinb = cosb_ref[...], sinb_ref[...]

    cqn = (cq * lax.rsqrt(jnp.mean(cq * cq, axis=-1, keepdims=True) + RMS_EPS) * gq_ref[...]).astype(BF16)
    ckvn = ckv * lax.rsqrt(jnp.mean(ckv * ckv, axis=-1, keepdims=True) + RMS_EPS) * gkv_ref[...]
    krope = misc * cosa + _partner(misc, 16) * sina
    rowsa_ref[:, 0:128] = ckvn
    rowsa_ref[:, 128:160] = krope[:, 0:32]
    kmla_ref[...] = jnp.concatenate([ckvn, krope], axis=1).astype(BF16)
    qlat = _nn(cqn, wabs_ref[...])
    qr = _nn(cqn, wr_ref[...])
    qrp = _nn(cqn, wrp_ref[...])
    for h in range(n_heads):
        sl = slice(h * LANES, (h + 1) * LANES)
        rope_h = qr[:, sl] * cosa + qrp[:, sl] * sina
        qmla_ref[h] = (jnp.concatenate([qlat[:, sl], rope_h], axis=1) * mla_scale).astype(BF16)

    rowsc_ref[...] = cmp_raw
    sel_k = sel_raw[:, 0:128] * cosb + _partner(sel_raw[:, 0:128], 32) * sinb
    win_k = win_raw[:, 0:128] * cosb + _partner(win_raw[:, 0:128], 32) * sinb
    rows_s = jnp.concatenate([sel_k, sel_raw[:, 128:256]], axis=1)
    rows_w = jnp.concatenate([win_k, win_raw[:, 128:256]], axis=1)
    rowss_ref[...] = rows_s
    rowsw_ref[...] = rows_w
    kvs_ref[...] = rows_s.astype(BF16)
    kvw_ref[...] = rows_w.astype(BF16)
    cos4 = jnp.concatenate([cosb] * 4, axis=1)
    sin4 = jnp.concatenate([sinb] * 4, axis=1)
    qb_rope = qb * cos4 + _partner(qb, 32) * sin4
    lane = _lane((x.shape[0], LANES))
    n_q = qraw_ref.shape[0]
    for hh in range(n_q):
        kvh = hh // (n_q // 2)
        blk = slice((hh // 2) * LANES, (hh // 2 + 1) * LANES)
        keep = (lane >= 64) if kvh == 1 else (lane < 64)
        for src, dst in ((qb, qraw_ref), (qb_rope, qrope_ref)):
            v = src[:, blk]
            if (hh % 2) != kvh:
                v = _swap64(v)
            dst[hh] = jnp.where(keep, v * nsa_scale, 0.0).astype(BF16)
    gates_ref[...] = jax.nn.sigmoid(misc)


def _proj(x, cosa, sina, cosb, sinb, ga, win_r, gq, gkv, wnope, wuk, wr, wrp, *, tm, mla_scale, nsa_scale):
    n, d = x.shape
    n_heads = wnope.shape[0]
    grid = (n // tm,)
    row = lambda i: (i, 0)
    full2 = lambda i: (0, 0)
    full3 = lambda i: (0, 0, 0)
    head_row = lambda i: (0, i, 0)
    in_specs = [
        pl.BlockSpec((tm, d), row),
        pl.BlockSpec(ga.shape, full2),
        pl.BlockSpec(win_r.shape, full2),
        pl.BlockSpec(gq.shape, full2),
        pl.BlockSpec(gkv.shape, full2),
        pl.BlockSpec(wnope.shape, full3),
        pl.BlockSpec(wuk.shape, full3),
        pl.BlockSpec(wr.shape, full2),
        pl.BlockSpec(wrp.shape, full2),
        pl.BlockSpec((tm, LANES), row), pl.BlockSpec((tm, LANES), row),
        pl.BlockSpec((tm, LANES), row), pl.BlockSpec((tm, LANES), row),
    ]
    out_shape = [
        jax.ShapeDtypeStruct((n, 160), F32),
        jax.ShapeDtypeStruct((n, 256), BF16),
        jax.ShapeDtypeStruct((n_heads, n, 256), BF16),
        jax.ShapeDtypeStruct((n, 256), F32),
        jax.ShapeDtypeStruct((n, 256), F32),
        jax.ShapeDtypeStruct((n, 256), F32),
        jax.ShapeDtypeStruct((n, 256), BF16),
        jax.ShapeDtypeStruct((n, 256), BF16),
        jax.ShapeDtypeStruct((8, n, LANES), BF16),
        jax.ShapeDtypeStruct((8, n, LANES), BF16),
        jax.ShapeDtypeStruct((n, LANES), F32),
    ]
    out_specs = [
        pl.BlockSpec((tm, 160), row),
        pl.BlockSpec((tm, 256), row),
        pl.BlockSpec((n_heads, tm, 256), head_row),
        pl.BlockSpec((tm, 256), row),
        pl.BlockSpec((tm, 256), row),
        pl.BlockSpec((tm, 256), row),
        pl.BlockSpec((tm, 256), row),
        pl.BlockSpec((tm, 256), row),
        pl.BlockSpec((8, tm, LANES), head_row),
        pl.BlockSpec((8, tm, LANES), head_row),
        pl.BlockSpec((tm, LANES), row),
    ]
    return pl.pallas_call(
        functools.partial(_proj_kernel, mla_scale=mla_scale, nsa_scale=nsa_scale), name="proj",
        grid=grid, in_specs=in_specs, out_specs=out_specs, out_shape=out_shape,
        scratch_shapes=[pltpu.VMEM((wnope.shape[1], n_heads * LANES), BF16)],
        compiler_params=_cparams(("arbitrary",)),
    )(x, ga, win_r, gq, gkv, wnope, wuk, wr, wrp, cosa, sina, cosb, sinb)


def _mla_prompt_kernel(q_ref, k_ref, wuv_ref, o_ref, m_scr, l_scr, acc_scr, *, tq, tk):
    i, j = pl.program_id(1), pl.program_id(2)
    nh = q_ref.shape[0]
    jmax = (i * tq + tq - 1) // tk

    @pl.when(j == 0)
    def _():
        m_scr[...] = jnp.full(m_scr.shape, NEG_INF, F32)
        l_scr[...] = jnp.zeros(l_scr.shape, F32)
        acc_scr[...] = jnp.zeros(acc_scr.shape, F32)

    @pl.when(j <= jmax)
    def _():
        q = q_ref[...].reshape(nh * tq, q_ref.shape[2])
        k = k_ref[...]
        s = _nt(q, k).reshape(nh, tq, tk)
        qpos = i * tq + lax.broadcasted_iota(jnp.int32, (1, tq, tk), 1)
        kpos = j * tk + lax.broadcasted_iota(jnp.int32, (1, tq, tk), 2)
        s = jnp.where(kpos <= qpos, s, NEG_INF).reshape(nh * tq, tk)
        m_old = m_scr[...]
        m_new = jnp.maximum(m_old, jnp.max(s, axis=-1, keepdims=True))
        alpha = jnp.exp(m_old - m_new)
        p = jnp.exp(s - m_new)
        l_scr[...] = l_scr[...] * alpha + jnp.sum(p, axis=-1, keepdims=True)
        acc_scr[...] = acc_scr[...] * alpha + _nn(p.astype(BF16), k[:, 0:LANES])
        m_scr[...] = m_new

    @pl.when(j == pl.num_programs(2) - 1)
    def _():
        o = (acc_scr[...] / l_scr[...]).astype(BF16).reshape(nh, tq, LANES)
        for m in range(nh // 2):
            pair = _nn(o[2 * m], wuv_ref[2 * m]) + _nn(o[2 * m + 1], wuv_ref[2 * m + 1])
            o_ref[:, m * LANES:(m + 1) * LANES] = pair.astype(o_ref.dtype)


def _mla_prompt(qmla, kmla, wuv, *, b, t, tq, tk):
    nh = qmla.shape[0]
    nq, nk = t // tq, t // tk
    out_w = nh * wuv.shape[2] // 2

    def kmap(bb, i, j):
        return (bb * nk + jnp.minimum(j, (i * tq + tq - 1) // tk), 0)

    return pl.pallas_call(
        functools.partial(_mla_prompt_kernel, tq=tq, tk=tk), name="mla_prompt",
        grid=(b, nq, nk),
        in_specs=[
            pl.BlockSpec((nh, tq, qmla.shape[2]), lambda bb, i, j: (0, bb * nq + i, 0)),
            pl.BlockSpec((tk, kmla.shape[1]), kmap),
            pl.BlockSpec(wuv.shape, lambda bb, i, j: (0, 0, 0)),
        ],
        out_specs=pl.BlockSpec((tq, out_w), lambda bb, i, j: (bb * nq + i, 0)),
        out_shape=jax.ShapeDtypeStruct((b * t, out_w), BF16),
        scratch_shapes=[pltpu.VMEM((nh * tq, 1), F32), pltpu.VMEM((nh * tq, 1), F32),
                        pltpu.VMEM((nh * tq, LANES), F32)],
        compiler_params=_cparams(("arbitrary", "arbitrary", "arbitrary")),
    )(qmla, kmla, wuv)


def _compress_kernel(tab_ref, src_ref, pe_ref, w1_ref, w2_ref, o_ref, buf, sem, *, pages, page_rows, cmp_block):
    g = pl.program_id(0)
    n = pl.num_programs(0)
    slot = g % 2

    def copies(step, sl):
        return [pltpu.make_async_copy(src_ref.at[tab_ref[step * pages + p], :, pl.ds(s * LANES, LANES)],
                                      buf.at[sl, s, pl.ds(p * page_rows, page_rows), :], sem.at[sl])
                for p in range(pages) for s in range(2)]

    @pl.when(g == 0)
    def _():
        for c in copies(0, 0):
            c.start()

    @pl.when(g + 1 < n)
    def _():
        for c in copies(g + 1, 1 - slot):
            c.start()

    for c in copies(g, slot):
        c.wait()

    nblk = pages * page_rows // cmp_block
    for s in range(2):
        acc = jnp.zeros((nblk, w1_ref.shape[3]), F32)
        for r in range(cmp_block):
            xr = buf[slot, s, pl.ds(r, nblk, stride=cmp_block), :] + pe_ref[r:r + 1, s * LANES:(s + 1) * LANES]
            acc = acc + _nn(xr.astype(BF16), w1_ref[s, r])
        hid = jax.nn.gelu(acc)
        o_ref[:, s * LANES:(s + 1) * LANES] = _nn(hid.astype(BF16), w2_ref[s])


def _compress(table, src_pages, pe_full, w1bd, w2bd, *, pages):
    n_pages = table.shape[0]
    page_rows, width = src_pages.shape[1], src_pages.shape[2]
    cmp_block = pe_full.shape[0]
    per_page = page_rows // cmp_block
    grid_spec = pltpu.PrefetchScalarGridSpec(
        num_scalar_prefetch=1, grid=(n_pages // pages,),
        in_specs=[
            pl.BlockSpec(memory_space=pl.ANY),
            pl.BlockSpec(pe_full.shape, lambda g, tab: (0, 0)),
            pl.BlockSpec(w1bd.shape, lambda g, tab: (0, 0, 0, 0)),
            pl.BlockSpec(w2bd.shape, lambda g, tab: (0, 0, 0)),
        ],
        out_specs=pl.BlockSpec((pages * per_page, width), lambda g, tab: (g, 0)),
        scratch_shapes=[pltpu.VMEM((2, 2, pages * page_rows, LANES), F32), pltpu.SemaphoreType.DMA((2,))],
    )
    return pl.pallas_call(
        functools.partial(_compress_kernel, pages=pages, page_rows=page_rows, cmp_block=cmp_block), name="compress",
        grid_spec=grid_spec,
        out_shape=jax.ShapeDtypeStruct((n_pages * per_page, width), F32),
        compiler_params=_cparams(("arbitrary",)),
    )(table, src_pages, pe_full, w1bd, w2bd)


def _cmp_softmax(q, kc, vc, valid):
    s = jnp.where(valid, _nt(q, kc), NEG_INF)
    m = jnp.max(s, axis=-1, keepdims=True)
    e = jnp.where(valid, jnp.exp(s - m), 0.0)
    l = jnp.sum(e, axis=-1, keepdims=True)
    p = e / jnp.where(l == 0.0, 1.0, l)
    return p, _nn(p.astype(BF16), vc)


def _select(imp, pos, n_sel, cmp_per_sel, cmp_block):
    nc = imp.shape[-1]
    lane = _lane(imp.shape)
    pair = imp
    for d in range(1, cmp_per_sel):
        pair = pair + pltpu.roll(imp, nc - d, imp.ndim - 1)
    j = lane // cmp_per_sel
    qb = pos // (cmp_per_sel * cmp_block)
    forced = (j == 0) | (j == qb) | (j == qb - 1)
    causal = j * (cmp_per_sel * cmp_block) <= pos
    score = jnp.where(causal, jnp.where(forced, FORCE_SCORE, pair), NEG_INF)
    score = jnp.where(lane % cmp_per_sel == 0, score, REMOVED)
    lane_f = lane.astype(F32)
    picks = []
    for _ in range(n_sel):
        m = jnp.max(score, axis=-1, keepdims=True)
        first = jnp.min(jnp.where(score == m, lane_f, 1e9), axis=-1, keepdims=True)
        hit = lane_f == first
        score = jnp.where(hit, REMOVED, score)
        picks.append(first)
    return picks


def _pair_heads(o_even, o_odd, kvh):
    lane = _lane(o_even.shape)
    a = o_even if kvh == 0 else _swap64(o_even)
    b = _swap64(o_odd) if kvh == 0 else o_odd
    return jnp.where(lane < 64, a, b)


def _gate_col(gates, hh, c):
    col = 32 + hh * 3 + c
    return gates[:, col:col + 1]


def _cmp_prompt_kernel(q_ref, bc_ref, g_ref, o_ref, sel_ref, *, tq, cmp_block, n_sel):
    i = pl.program_id(1)
    nc = bc_ref.shape[0]
    gates = g_ref[...]
    bc = bc_ref[...]
    kc = bc[:, 0:LANES].astype(BF16)
    vc = bc[:, LANES:2 * LANES].astype(BF16)
    pos = i * tq + lax.broadcasted_iota(jnp.int32, (tq, 1), 0)
    lane = _lane((tq, nc))
    valid = lane * cmp_block + (cmp_block - 1) <= pos
    cmp_per_sel = SEL_BLOCK // cmp_block
    n_q = q_ref.shape[0]
    grp = n_q // 2
    for kvh in range(2):
        imp = jnp.zeros((tq, nc), F32)
        outs = []
        for g in range(grp):
            hh = kvh * grp + g
            p, o = _cmp_softmax(q_ref[hh], kc, vc, valid)
            imp = imp + p
            outs.append(o * _gate_col(gates, hh, 0))
        for g2 in range(grp // 2):
            blk = kvh * (grp // 2) + g2
            o_ref[:, blk * LANES:(blk + 1) * LANES] = _pair_heads(outs[2 * g2], outs[2 * g2 + 1], kvh)
        picks = _select(imp, pos, n_sel, cmp_per_sel, cmp_block)
        lane_f = lane.astype(F32)
        selm = jnp.zeros((tq, nc), F32)
        for first in picks:
            selm = jnp.where(lane_f == first, 1.0, selm)
        sel_ref[kvh] = selm.astype(BF16)


def _cmp_prompt(qraw, blocks_c, gates, *, b, t, tq, cmp_block):
    n_q = qraw.shape[0]
    nq = t // tq
    nc = t // cmp_block
    n_sel = min(N_SEL, t // SEL_BLOCK)
    return pl.pallas_call(
        functools.partial(_cmp_prompt_kernel, tq=tq, cmp_block=cmp_block, n_sel=n_sel), name="cmp_prompt",
        grid=(b, nq),
        in_specs=[
            pl.BlockSpec((n_q, tq, LANES), lambda bb, i: (0, bb * nq + i, 0)),
            pl.BlockSpec((nc, blocks_c.shape[1]), lambda bb, i: (bb, 0)),
            pl.BlockSpec((tq, LANES), lambda bb, i: (bb * nq + i, 0)),
        ],
        out_specs=[
            pl.BlockSpec((tq, n_q * 64), lambda bb, i: (bb * nq + i, 0)),
            pl.BlockSpec((2, tq, nc), lambda bb, i: (0, bb * nq + i, 0)),
        ],
        out_shape=[jax.ShapeDtypeStruct((b * t, n_q * 64), F32),
                   jax.ShapeDtypeStruct((2, b * t, nc), BF16)],
        compiler_params=_cparams(("arbitrary", "arbitrary")),
    )(qraw, blocks_c, gates)


def _sel_prompt_kernel(q_ref, k_ref, v_ref, sel_ref, g_ref, o_ref, m_scr, l_scr, acc_scr, *, tq, tk, cmp_block):
    kvh, i, j = pl.program_id(1), pl.program_id(2), pl.program_id(3)
    grp = q_ref.shape[0]
    nc = sel_ref.shape[2]
    jmax = (i * tq + tq - 1) // tk
    cmp_per_sel = SEL_BLOCK // cmp_block

    @pl.when(j == 0)
    def _():
        m_scr[...] = jnp.full(m_scr.shape, NEG_INF, F32)
        l_scr[...] = jnp.zeros(l_scr.shape, F32)
        acc_scr[...] = jnp.zeros(acc_scr.shape, F32)

    @pl.when(j <= jmax)
    def _():
        q = q_ref[...].reshape(grp * tq, LANES)
        s = _nt(q, k_ref[...]).reshape(grp, tq, tk)
        row_l = lax.broadcasted_iota(jnp.int32, (nc, tk), 0)
        key = j * tk + lax.broadcasted_iota(jnp.int32, (nc, tk), 1)
        expand = (row_l == (key // SEL_BLOCK) * cmp_per_sel).astype(BF16)
        msel = _nn(sel_ref[0], expand)
        qpos = i * tq + lax.broadcasted_iota(jnp.int32, (tq, tk), 0)
        kpos = j * tk + lax.broadcasted_iota(jnp.int32, (tq, tk), 1)
        valid = ((msel > 0.5) & (kpos <= qpos))[None]
        s = jnp.where(valid, s, NEG_INF)
        m_old = m_scr[...].reshape(grp, tq, 1)
        m_new = jnp.maximum(m_old, jnp.max(s, axis=-1, keepdims=True))
        alpha = jnp.exp(m_old - m_new)
        p = jnp.where(valid, jnp.exp(s - m_new), 0.0)
        l_scr[...] = (l_scr[...].reshape(grp, tq, 1) * alpha + jnp.sum(p, axis=-1, keepdims=True)).reshape(grp * tq, 1)
        pv = _nn(p.reshape(grp * tq, tk).astype(BF16), v_ref[...])
        acc_scr[...] = acc_scr[...] * alpha.reshape(grp * tq, 1) + pv
        m_scr[...] = m_new.reshape(grp * tq, 1)

    @pl.when(j == pl.num_programs(3) - 1)
    def _():
        l = l_scr[...]
        o = (acc_scr[...] / jnp.where(l == 0.0, 1.0, l)).reshape(grp, tq, LANES)
        gates = g_ref[...]
        for g2 in range(grp // 2):
            for kk in range(2):
                @pl.when(kvh == kk)
                def _():
                    a = o[2 * g2] * _gate_col(gates, kk * grp + 2 * g2, 1)
                    b2 = o[2 * g2 + 1] * _gate_col(gates, kk * grp + 2 * g2 + 1, 1)
                    o_ref[:, g2 * LANES:(g2 + 1) * LANES] = _pair_heads(a, b2, kk)


def _sel_prompt(qrope, kvs, selmask, gates, *, b, t, tq, tk, cmp_block):
    n_q = qrope.shape[0]
    grp = n_q // 2
    nq, nk = t // tq, t // tk
    nc = selmask.shape[2]

    def kv_map(lane_blk):
        def f(bb, kvh, i, j):
            return (bb * nk + jnp.minimum(j, (i * tq + tq - 1) // tk), lane_blk)
        return f

    return pl.pallas_call(
        functools.partial(_sel_prompt_kernel, tq=tq, tk=tk, cmp_block=cmp_block), name="sel_prompt",
        grid=(b, 2, nq, nk),
        in_specs=[
            pl.BlockSpec((grp, tq, LANES), lambda bb, kvh, i, j: (kvh, bb * nq + i, 0)),
            pl.BlockSpec((tk, LANES), kv_map(0)),
            pl.BlockSpec((tk, LANES), kv_map(1)),
            pl.BlockSpec((1, tq, nc), lambda bb, kvh, i, j: (kvh, bb * nq + i, 0)),
            pl.BlockSpec((tq, LANES), lambda bb, kvh, i, j: (bb * nq + i, 0)),
        ],
        out_specs=pl.BlockSpec((tq, grp * 64), lambda bb, kvh, i, j: (bb * nq + i, kvh)),
        out_shape=jax.ShapeDtypeStruct((b * t, n_q * 64), F32),
        scratch_shapes=[pltpu.VMEM((grp * tq, 1), F32), pltpu.VMEM((grp * tq, 1), F32),
                        pltpu.VMEM((grp * tq, LANES), F32)],
        compiler_params=_cparams(("arbitrary",) * 4),
    )(qrope, kvs, kvs, selmask, gates)


def _win_prompt_kernel(q_ref, k0, k1, k2, v0, v1, v2, g_ref, o_ref, *, tq, window):
    kvh, i = pl.program_id(1), pl.program_id(2)
    grp = q_ref.shape[0]
    q = q_ref[...].reshape(grp * tq, LANES)
    qpos = i * tq + lax.broadcasted_iota(jnp.int32, (tq, tq), 0)
    col = lax.broadcasted_iota(jnp.int32, (tq, tq), 1)
    ss, vs = [], []
    for d, (kr, vr) in enumerate(((k0, v0), (k1, v1), (k2, v2))):
        blk = i - 2 + d
        kpos = blk * tq + col
        valid = ((kpos >= 0) & (kpos <= qpos) & (kpos > qpos - window))[None]
        s = _nt(q, kr[...]).reshape(grp, tq, tq)
        ss.append(jnp.where(valid, s, NEG_INF))
        vs.append(valid)
    s = jnp.concatenate(ss, axis=2)
    valid = jnp.concatenate(vs, axis=2)
    m = jnp.max(s, axis=-1, keepdims=True)
    e = jnp.where(valid, jnp.exp(s - m), 0.0)
    l = jnp.sum(e, axis=-1, keepdims=True)
    p = (e / jnp.where(l == 0.0, 1.0, l)).reshape(grp * tq, 3 * tq).astype(BF16)
    o = _nn(p[:, 0:tq], v0[...]) + _nn(p[:, tq:2 * tq], v1[...]) + _nn(p[:, 2 * tq:3 * tq], v2[...])
    o = o.reshape(grp, tq, LANES)
    gates = g_ref[...]
    for g2 in range(grp // 2):
        for kk in range(2):
            @pl.when(kvh == kk)
            def _():
                a = o[2 * g2] * _gate_col(gates, kk * grp + 2 * g2, 2)
                b2 = o[2 * g2 + 1] * _gate_col(gates, kk * grp + 2 * g2 + 1, 2)
                o_ref[:, g2 * LANES:(g2 + 1) * LANES] = _pair_heads(a, b2, kk)


def _win_prompt(qrope, kvw, gates, *, b, t, tq, window):
    assert window == 2 * tq
    n_q = qrope.shape[0]
    grp = n_q // 2
    nq = t // tq

    def kv_map(d, lane_blk):
        def f(bb, kvh, i):
            return (bb * nq + jnp.maximum(i - 2 + d, 0), lane_blk)
        return f

    kv_specs = [pl.BlockSpec((tq, LANES), kv_map(d, lb)) for lb in (0, 1) for d in range(3)]
    return pl.pallas_call(
        functools.partial(_win_prompt_kernel, tq=tq, window=window), name="win_prompt",
        grid=(b, 2, nq),
        in_specs=[pl.BlockSpec((grp, tq, LANES), lambda bb, kvh, i: (kvh, bb * nq + i, 0))] + kv_specs
                 + [pl.BlockSpec((tq, LANES), lambda bb, kvh, i: (bb * nq + i, 0))],
        out_specs=pl.BlockSpec((tq, grp * 64), lambda bb, kvh, i: (bb * nq + i, kvh)),
        out_shape=jax.ShapeDtypeStruct((b * t, n_q * 64), F32),
        compiler_params=_cparams(("arbitrary",) * 3),
    )(qrope, kvw, kvw, kvw, kvw, kvw, kvw, gates)


def _mla_sample_kernel(pt_ref, q_ref, knew_ref, wuv_ref, cache_ref, o_ref, buf, sem, *,
                       pages, n_chunks, page_rows, ts, lat):
    b = pl.program_id(0)
    nb = pl.num_programs(0)
    rows = q_ref.shape[1]

    def copies(step, sl):
        return [pltpu.make_async_copy(cache_ref.at[pt_ref[step * pages + p]],
                                      buf.at[sl, pl.ds(p * page_rows, page_rows), :], sem.at[sl])
                for p in range(pages)]

    @pl.when(b == 0)
    def _():
        for c in copies(0, 0):
            c.start()

    q = q_ref[0]
    q_lat, q_rope = q[:, 0:LANES], q[:, LANES:lat]

    def chunk(c, carry):
        m_old, l_old, acc = carry
        step = b * n_chunks + c
        slot = step % 2

        @pl.when(step + 1 < nb * n_chunks)
        def _():
            for cp in copies(step + 1, 1 - slot):
                cp.start()

        for cp in copies(step, slot):
            cp.wait()
        kb = buf[slot].astype(BF16)
        s = _nt(q_lat, kb[:, 0:LANES]) + _nt(q_rope, kb[:, LANES:lat])
        m_new = jnp.maximum(m_old, jnp.max(s, axis=-1, keepdims=True))
        alpha = jnp.exp(m_old - m_new)
        p = jnp.exp(s - m_new)
        l_new = l_old * alpha + jnp.sum(p, axis=-1, keepdims=True)
        acc = acc * alpha + _nn(p.astype(BF16), kb[:, 0:LANES])
        return m_new, l_new, acc

    init = (jnp.full((rows, 1), NEG_INF, F32), jnp.zeros((rows, 1), F32), jnp.zeros((rows, LANES), F32))
    m_old, l_old, acc = lax.fori_loop(0, n_chunks, chunk, init)

    kn = knew_ref[0]
    s = _nt(q, kn)
    tok = lax.broadcasted_iota(jnp.int32, s.shape, 0) % ts
    col = lax.broadcasted_iota(jnp.int32, s.shape, 1)
    valid = (col <= tok) & (col < ts)
    s = jnp.where(valid, s, NEG_INF)
    m_new = jnp.maximum(m_old, jnp.max(s, axis=-1, keepdims=True))
    alpha = jnp.exp(m_old - m_new)
    p = jnp.where(valid, jnp.exp(s - m_new), 0.0)
    l_new = l_old * alpha + jnp.sum(p, axis=-1, keepdims=True)
    acc = acc * alpha + _nn(p.astype(BF16), kn[:, 0:LANES])
    o = (acc / l_new).astype(BF16)
    full = _nn(o, wuv_ref[...])
    nh = rows // ts
    vw = wuv_ref.shape[1] // nh
    lane_head = _lane((ts, full.shape[1])) // vw
    out = jnp.zeros((ts, full.shape[1]), F32)
    for h in range(nh):
        out = jnp.where(lane_head == h, full[h * ts:(h + 1) * ts, :], out)
    o_ref[0] = out


def _mla_sample(page_table, q_s, knew, wuv_flat, cache, *, pages, ts):
    bs, n_pages = page_table.shape
    page_rows, lat = cache.shape[1], cache.shape[2]
    n_chunks = n_pages // pages
    rows = q_s.shape[1]
    width = q_s.shape[2]
    grid_spec = pltpu.PrefetchScalarGridSpec(
        num_scalar_prefetch=1, grid=(bs,),
        in_specs=[
            pl.BlockSpec((1, rows, width), lambda b, pt: (b, 0, 0)),
            pl.BlockSpec((1,) + knew.shape[1:], lambda b, pt: (b, 0, 0)),
            pl.BlockSpec(wuv_flat.shape, lambda b, pt: (0, 0)),
            pl.BlockSpec(memory_space=pl.ANY),
        ],
        out_specs=pl.BlockSpec((1, ts, wuv_flat.shape[1]), lambda b, pt: (b, 0, 0)),
        scratch_shapes=[pltpu.VMEM((2, pages * page_rows, lat), F32), pltpu.SemaphoreType.DMA((2,))],
    )
    return pl.pallas_call(
        functools.partial(_mla_sample_kernel, pages=pages, n_chunks=n_chunks, page_rows=page_rows, ts=ts, lat=lat),
        name="mla_sample",
        grid_spec=grid_spec,
        out_shape=jax.ShapeDtypeStruct((bs, ts, wuv_flat.shape[1]), F32),
        compiler_params=_cparams(("arbitrary",)),
    )(page_table.reshape(-1), q_s, knew, wuv_flat, cache)


def _cmp_sample_kernel(q_ref, bc_ref, g_ref, o_ref, idx_ref, *, ts, past, cmp_block, n_sel, grp):
    q = q_ref[0]
    bc = bc_ref[0]
    nc = bc.shape[0]
    kc = bc[:, 0:LANES].astype(BF16)
    vc = bc[:, LANES:2 * LANES].astype(BF16)
    gates = g_ref[0]
    rows = ts * grp
    cmp_per_sel = SEL_BLOCK // cmp_block
    lane = _lane((rows, nc))
    pos_r = past + lax.broadcasted_iota(jnp.int32, (rows, 1), 0) // grp
    valid = lane * cmp_block + (cmp_block - 1) <= pos_r
    pos_t = past + lax.broadcasted_iota(jnp.int32, (ts, 1), 0)
    lane_o = _lane((ts, LANES))
    for kvh in range(2):
        p, o = _cmp_softmax(q[kvh * rows:(kvh + 1) * rows], kc, vc, valid)
        imp = jnp.sum(p.reshape(ts, grp, nc), axis=1)
        picks = _select(imp, pos_t, n_sel, cmp_per_sel, cmp_block)
        idx = jnp.zeros((ts, LANES), jnp.int32)
        for it, first in enumerate(picks):
            idx = jnp.where(lane_o == it, first.astype(jnp.int32) // cmp_per_sel, idx)
        idx_ref[0, kvh * ts:(kvh + 1) * ts, :] = idx
        o3 = o.reshape(ts, grp, LANES)
        for g2 in range(grp // 2):
            hh = kvh * grp + 2 * g2
            a = o3[:, 2 * g2, :] * _gate_col(gates, hh, 0)
            b2 = o3[:, 2 * g2 + 1, :] * _gate_col(gates, hh + 1, 0)
            blk = kvh * (grp // 2) + g2
            o_ref[0, :, blk * LANES:(blk + 1) * LANES] = _pair_heads(a, b2, kvh)


def _cmp_sample(q_s, blocks_c, gates_s, *, ts, past, cmp_block, n_sel, grp):
    bs = q_s.shape[0]
    return pl.pallas_call(
        functools.partial(_cmp_sample_kernel, ts=ts, past=past, cmp_block=cmp_block, n_sel=n_sel, grp=grp),
        name="cmp_sample",
        grid=(bs,),
        in_specs=[
            pl.BlockSpec((1,) + q_s.shape[1:], lambda b: (b, 0, 0)),
            pl.BlockSpec((1,) + blocks_c.shape[1:], lambda b: (b, 0, 0)),
            pl.BlockSpec((1,) + gates_s.shape[1:], lambda b: (b, 0, 0)),
        ],
        out_specs=[
            pl.BlockSpec((1, ts, 2 * grp * 64), lambda b: (b, 0, 0)),
            pl.BlockSpec((1, 2 * ts, LANES), lambda b: (b, 0, 0)),
        ],
        out_shape=[jax.ShapeDtypeStruct((bs, ts, 2 * grp * 64), F32),
                   jax.ShapeDtypeStruct((bs, 2 * ts, LANES), jnp.int32)],
        compiler_params=_cparams(("arbitrary",)),
    )(q_s, blocks_c, gates_s)


def _selwin_sample_kernel(idx_ref, pt_ref, q_ref, win_ref, wnew_ref, g_ref, cache_ref, snew_ref,
                          osel_ref, owin_ref, buf, sem, *,
                          ts, grp, n_sel, past_blocks, per_page, n_pages):
    b = pl.program_id(0)
    nb = pl.num_programs(0)
    slot = b % 2
    n_slots = 2 * ts * n_sel

    def issue(bb, sl, wait):
        for kt in range(2 * ts):
            for it in range(n_sel):
                blk = idx_ref[(bb * 2 * ts + kt) * n_sel + it]
                dst = buf.at[sl, kt, pl.ds(it * SEL_BLOCK, SEL_BLOCK), :]
                if wait:
                    pltpu.make_async_copy(snew_ref.at[bb], dst, sem.at[sl]).wait()
                    continue
                in_past = blk < past_blocks

                @pl.when(in_past)
                def _():
                    phys = pt_ref[bb * n_pages + blk // per_page]
                    src = cache_ref.at[phys, pl.ds((blk % per_page) * SEL_BLOCK, SEL_BLOCK), :]
                    pltpu.make_async_copy(src, dst, sem.at[sl]).start()

                @pl.when(jnp.logical_not(in_past))
                def _():
                    pltpu.make_async_copy(snew_ref.at[bb], dst, sem.at[sl]).start()

    @pl.when(b == 0)
    def _():
        issue(0, 0, False)

    @pl.when(b + 1 < nb)
    def _():
        issue(b + 1, 1 - slot, False)

    issue(b, slot, True)

    q = q_ref[0]
    gates = g_ref[0]
    nkeys = n_sel * SEL_BLOCK
    lane_k = _lane((1, nkeys))
    rows = ts * grp
    for kvh in range(2):
        outs = []
        for t in range(ts):
            kt = kvh * ts + t
            kv = buf[slot, kt]
            kk = kv[:, 0:LANES].astype(BF16)
            vv = kv[:, LANES:2 * LANES].astype(BF16)
            new_ok = jnp.where((lane_k % SEL_BLOCK) <= t, 1, 0)
            kval = jnp.zeros((1, nkeys), jnp.int32)
            for it in range(n_sel):
                blk = idx_ref[(b * 2 * ts + kt) * n_sel + it]
                ok = jnp.maximum(jnp.where(blk < past_blocks, 1, 0), new_ok)
                kval = jnp.where((lane_k // SEL_BLOCK) == it, ok, kval)
            kvalid = kval > 0
            q4 = q[kt * grp:(kt + 1) * grp]
            s = jnp.where(kvalid, _nt(q4, kk), NEG_INF)
            m = jnp.max(s, axis=-1, keepdims=True)
            e = jnp.where(kvalid, jnp.exp(s - m), 0.0)
            l = jnp.sum(e, axis=-1, keepdims=True)
            p = e / jnp.where(l == 0.0, 1.0, l)
            outs.append(_nn(p.astype(BF16), vv))
        o3 = jnp.stack(outs, axis=0)
        for g2 in range(grp // 2):
            hh = kvh * grp + 2 * g2
            a = o3[:, 2 * g2, :] * _gate_col(gates, hh, 1)
            b2 = o3[:, 2 * g2 + 1, :] * _gate_col(gates, hh + 1, 1)
            blk_o = kvh * (grp // 2) + g2
            osel_ref[0, :, blk_o * LANES:(blk_o + 1) * LANES] = _pair_heads(a, b2, kvh)

        wb = win_ref[0]
        wlen = wb.shape[0]
        wk = wb[:, 0:LANES].astype(BF16)
        wv = wb[:, LANES:2 * LANES].astype(BF16)
        wn = wnew_ref[0]
        nk_ = wn[:, 0:LANES].astype(BF16)
        nv_ = wn[:, LANES:2 * LANES].astype(BF16)
        qk = q[kvh * rows:(kvh + 1) * rows]
        tok = lax.broadcasted_iota(jnp.int32, (rows, 1), 0) // grp
        v_old = _lane((rows, wlen)) > tok
        col_n = _lane((rows, wn.shape[0]))
        v_new = (col_n <= tok) & (col_n < ts)
        s_old = jnp.where(v_old, _nt(qk, wk), NEG_INF)
        s_new = jnp.where(v_new, _nt(qk, nk_), NEG_INF)
        m = jnp.maximum(jnp.max(s_old, axis=-1, keepdims=True), jnp.max(s_new, axis=-1, keepdims=True))
        e_old = jnp.where(v_old, jnp.exp(s_old - m), 0.0)
        e_new = jnp.where(v_new, jnp.exp(s_new - m), 0.0)
        l = jnp.sum(e_old, axis=-1, keepdims=True) + jnp.sum(e_new, axis=-1, keepdims=True)
        inv = 1.0 / jnp.where(l == 0.0, 1.0, l)
        ow = (_nn((e_old * inv).astype(BF16), wv) + _nn((e_new * inv).astype(BF16), nv_)).reshape(ts, grp, LANES)
        for g2 in range(grp // 2):
            hh = kvh * grp + 2 * g2
            a = ow[:, 2 * g2, :] * _gate_col(gates, hh, 2)
            b2 = ow[:, 2 * g2 + 1, :] * _gate_col(gates, hh + 1, 2)
            blk_o = kvh * (grp // 2) + g2
            owin_ref[0, :, blk_o * LANES:(blk_o + 1) * LANES] = _pair_heads(a, b2, kvh)


def _selwin_sample(idx_flat, page_table, q_s, win_buf, wnew, gates_s, cache_sel, snew, *,
                   ts, grp, n_sel, past_blocks, per_page):
    bs, n_pages = page_table.shape
    width = cache_sel.shape[2]
    grid_spec = pltpu.PrefetchScalarGridSpec(
        num_scalar_prefetch=2, grid=(bs,),
        in_specs=[
            pl.BlockSpec((1,) + q_s.shape[1:], lambda b, i, p: (b, 0, 0)),
            pl.BlockSpec((1,) + win_buf.shape[1:], lambda b, i, p: (b, 0, 0)),
            pl.BlockSpec((1,) + wnew.shape[1:], lambda b, i, p: (b, 0, 0)),
            pl.BlockSpec((1,) + gates_s.shape[1:], lambda b, i, p: (b, 0, 0)),
            pl.BlockSpec(memory_space=pl.ANY),
            pl.BlockSpec(memory_space=pl.ANY),
        ],
        out_specs=[pl.BlockSpec((1, ts, 2 * grp * 64), lambda b, i, p: (b, 0, 0)),
                   pl.BlockSpec((1, ts, 2 * grp * 64), lambda b, i, p: (b, 0, 0))],
        scratch_shapes=[pltpu.VMEM((2, 2 * ts, n_sel * SEL_BLOCK, width), F32), pltpu.SemaphoreType.DMA((2,))],
    )
    return pl.pallas_call(
        functools.partial(_selwin_sample_kernel, ts=ts, grp=grp, n_sel=n_sel, past_blocks=past_blocks,
                          per_page=per_page, n_pages=n_pages),
        name="selwin_sample", grid_spec=grid_spec,
        out_shape=[jax.ShapeDtypeStruct((bs, ts, 2 * grp * 64), F32)] * 2,
        compiler_params=_cparams(("arbitrary",)),
    )(idx_flat, page_table.reshape(-1), q_s, win_buf, wnew, gates_s, cache_sel, snew)


def _outproj_kernel(h_ref, oa_ref, oc_ref, os_ref, ow_ref, wo_ref, gf_ref, wrh_ref, wrl_ref, br_ref,
                    h1_ref, xn_ref, e_ref, gw_ref, *, top_k):
    half = oa_ref.shape[1]
    ob = (oc_ref[...] + os_ref[...] + ow_ref[...]).astype(BF16)
    h1 = h_ref[...] + _nn(oa_ref[...], wo_ref[0:half, :]) + _nn(ob, wo_ref[half:2 * half, :])
    h1_ref[...] = h1
    xn = h1 * lax.rsqrt(jnp.mean(h1 * h1, axis=-1, keepdims=True) + RMS_EPS) * gf_ref[...]
    xn_ref[...] = xn
    hi, lo = _split(xn)
    logits = _nn(hi, wrh_ref[...]) + _nn(lo, wrh_ref[...]) + _nn(hi, wrl_ref[...]) + br_ref[...]
    lane = _lane(logits.shape)
    lane_f = lane.astype(F32)
    score = logits
    tops, es = [], []
    for _ in range(top_k):
        m = jnp.max(score, axis=-1, keepdims=True)
        first = jnp.min(jnp.where(score == m, lane_f, 1e9), axis=-1, keepdims=True)
        score = jnp.where(lane_f == first, REMOVED, score)
        tops.append(m)
        es.append(first)
    ws = [jnp.exp(t - tops[0]) for t in tops]
    tot = functools.reduce(lambda a, b2: a + b2, ws)
    e_out = jnp.zeros(logits.shape, jnp.int32)
    w_out = jnp.zeros(logits.shape, F32)
    for k in range(top_k):
        e_out = jnp.where(lane == k, es[k].astype(jnp.int32), e_out)
        w_out = jnp.where(lane == k, ws[k] / tot, w_out)
    e_ref[...] = e_out
    gw_ref[...] = w_out


def _outproj(h, oa, oc, os_, ow, wo, gf, wrh, wrl, br, *, tm, top_k):
    n, d = h.shape
    row = lambda i: (i, 0)
    full = lambda i: (0, 0)
    return pl.pallas_call(
        functools.partial(_outproj_kernel, top_k=top_k), name="outproj",
        grid=(n // tm,),
        in_specs=[pl.BlockSpec((tm, d), row), pl.BlockSpec((tm, oa.shape[1]), row),
                  pl.BlockSpec((tm, oc.shape[1]), row), pl.BlockSpec((tm, oc.shape[1]), row),
                  pl.BlockSpec((tm, oc.shape[1]), row),
                  pl.BlockSpec(wo.shape, full), pl.BlockSpec(gf.shape, full),
                  pl.BlockSpec(wrh.shape, full), pl.BlockSpec(wrl.shape, full), pl.BlockSpec(br.shape, full)],
        out_specs=[pl.BlockSpec((tm, d), row), pl.BlockSpec((tm, d), row),
                   pl.BlockSpec((tm, LANES), row), pl.BlockSpec((tm, LANES), row)],
        out_shape=[jax.ShapeDtypeStruct((n, d), F32), jax.ShapeDtypeStruct((n, d), F32),
                   jax.ShapeDtypeStruct((n, LANES), jnp.int32), jax.ShapeDtypeStruct((n, LANES), F32)],
        compiler_params=_cparams(("parallel",)),
    )(h, oa, oc, os_, ow, wo, gf, wrh, wrl, br)


def _gather_kernel(idx_ref, src_ref, o_ref, sem, *, rows):
    g = pl.program_id(0)

    def start(r, c):
        pltpu.make_async_copy(src_ref.at[pl.ds(idx_ref[g * rows + r], 1)], o_ref.at[pl.ds(r, 1)], sem.at[0]).start()
        return c

    def wait(r, c):
        pltpu.make_async_copy(src_ref.at[pl.ds(0, 1)], o_ref.at[pl.ds(r, 1)], sem.at[0]).wait()
        return c

    lax.fori_loop(0, rows, start, 0)
    lax.fori_loop(0, rows, wait, 0)


def _gather_rows(idx, src, *, rows):
    n = idx.shape[0]
    d = src.shape[1]
    grid_spec = pltpu.PrefetchScalarGridSpec(
        num_scalar_prefetch=1, grid=(n // rows,),
        in_specs=[pl.BlockSpec(memory_space=pl.ANY)],
        out_specs=pl.BlockSpec((rows, d), lambda g, idx_: (g, 0)),
        scratch_shapes=[pltpu.SemaphoreType.DMA((1,))],
    )
    return pl.pallas_call(
        functools.partial(_gather_kernel, rows=rows), name="gather_rows",
        grid_spec=grid_spec,
        out_shape=jax.ShapeDtypeStruct((n, d), src.dtype),
        compiler_params=_cparams(("arbitrary",)),
    )(idx, src)


def _expert_kernel(be_ref, nu_ref, x_ref, gw_ref, w1_ref, b1_ref, w2_ref, b2_ref, o_ref, w1s, w2s, *, d_ff):
    i = pl.program_id(0)
    prev = be_ref[jnp.maximum(i - 1, 0)]

    @pl.when((i == 0) | (be_ref[i] != prev))
    def _():
        w1s[...] = w1_ref[0].astype(BF16)
        w2s[...] = w2_ref[0].astype(BF16)

    @pl.when(i < nu_ref[0])
    def _():
        hb = _nn(x_ref[...].astype(BF16), w1s[...]) + b1_ref[0]
        g = jnp.minimum(hb[:, 0:d_ff], SWIGLU_LIMIT)
        u = jnp.clip(hb[:, d_ff:2 * d_ff], -SWIGLU_LIMIT, SWIGLU_LIMIT)
        act = (u + 1.0) * g * jax.nn.sigmoid(SWIGLU_ALPHA * g)
        y = _nn(act.astype(BF16), w2s[...]) + b2_ref[0]
        o_ref[...] = y * gw_ref[...]

    @pl.when(i >= nu_ref[0])
    def _():
        o_ref[...] = jnp.zeros(o_ref.shape, F32)


def _experts(block_e, n_used, x_sorted, gate_sorted, w1, b1, w2, b2, *, bm):
    r, d = x_sorted.shape
    d_ff = w2.shape[1]
    grid_spec = pltpu.PrefetchScalarGridSpec(
        num_scalar_prefetch=2, grid=(r // bm,),
        in_specs=[
            pl.BlockSpec((bm, d), lambda i, be, nu: (i, 0)),
            pl.BlockSpec((bm, 1), lambda i, be, nu: (i, 0)),
            pl.BlockSpec((1,) + w1.shape[1:], lambda i, be, nu: (be[i], 0, 0)),
            pl.BlockSpec((1, 1, b1.shape[2]), lambda i, be, nu: (be[i], 0, 0)),
            pl.BlockSpec((1,) + w2.shape[1:], lambda i, be, nu: (be[i], 0, 0)),
            pl.BlockSpec((1, 1, b2.shape[2]), lambda i, be, nu: (be[i], 0, 0)),
        ],
        out_specs=pl.BlockSpec((bm, d), lambda i, be, nu: (i, 0)),
        scratch_shapes=[pltpu.VMEM(w1.shape[1:], BF16), pltpu.VMEM(w2.shape[1:], BF16)],
    )
    return pl.pallas_call(
        functools.partial(_expert_kernel, d_ff=d_ff), name="experts",
        grid_spec=grid_spec,
        out_shape=jax.ShapeDtypeStruct((r, d), F32),
        compiler_params=_cparams(("arbitrary",)),
    )(block_e, n_used, x_sorted, gate_sorted, w1, b1, w2, b2)


def _final_kernel(h1_ref, y_ref, p_ref, wp_ref, gp_ref, wg_ref, gfin_ref, o_ref, *, top_k):
    d = h1_ref.shape[1]
    h2 = h1_ref[...]
    for k in range(top_k):
        h2 = h2 + y_ref[:, k * d:(k + 1) * d]
    hn = h2 * lax.rsqrt(jnp.mean(h2 * h2, axis=-1, keepdims=True) + RMS_EPS) * gp_ref[...]
    gate = jax.nn.sigmoid(_nn(hn.astype(BF16), wg_ref[...]))
    h3 = h2 + _nn(p_ref[...].astype(BF16), wp_ref[...]) * gate
    o_ref[...] = h3 * lax.rsqrt(jnp.mean(h3 * h3, axis=-1, keepdims=True) + RMS_EPS) * gfin_ref[...]


def _final(h1, y4, p, wp, gp, wg, gfin, *, tm, top_k):
    n, d = h1.shape
    row = lambda i: (i, 0)
    full = lambda i: (0, 0)
    return pl.pallas_call(
        functools.partial(_final_kernel, top_k=top_k), name="final",
        grid=(n // tm,),
        in_specs=[pl.BlockSpec((tm, d), row), pl.BlockSpec((tm, top_k * d), row), pl.BlockSpec((tm, p.shape[1]), row),
                  pl.BlockSpec(wp.shape, full), pl.BlockSpec(gp.shape, full), pl.BlockSpec(wg.shape, full),
                  pl.BlockSpec(gfin.shape, full)],
        out_specs=pl.BlockSpec((tm, d), row),
        out_shape=jax.ShapeDtypeStruct((n, d), F32),
        compiler_params=_cparams(("parallel",)),
    )(h1, y4, p, wp, gp, wg, gfin)


def _rope_tables(pos, dim, reps, width):
    inv = ROPE_THETA ** (-jnp.arange(0, dim, 2, dtype=F32) / dim)
    ang = pos.astype(F32)[:, None] * inv[None, :]
    cos, sin = jnp.cos(ang), jnp.sin(ang)
    c = jnp.tile(jnp.concatenate([cos, cos], axis=1), (1, reps))
    s = jnp.tile(jnp.concatenate([-sin, sin], axis=1), (1, reps))
    pad = width - c.shape[1]
    return jnp.pad(c, ((0, 0), (0, pad))), jnp.pad(s, ((0, 0), (0, pad)))


def _tile_rows(n, pref):
    for t in (pref, 256, 128, 64, 32, 16, 8):
        if t <= pref and n % t == 0:
            return t
    return n


def kernel(x_prompt, x_sample, cache_mla, cache_nsa_cmp, cache_nsa_sel, state_nsa_win, page_table, p_prompt, p_sample,
           g_attn, w_in, g_q, w_uq, g_kv, w_uk, w_uv, cmp_pe, w_cmp1, w_cmp2, w_o, g_ffn, w_router, b_router,
           w_e_in, b_e_in, w_e_out, b_e_out, g_ple, w_ple, w_ple_gate, g_final):
    depth = g_attn.shape[0]
    assert depth == 1
    b, t, d = x_prompt.shape
    bs, ts, _ = x_sample.shape
    n_pages = page_table.shape[1]
    page_rows = cache_mla.shape[2]
    past = n_pages * page_rows
    lat = cache_mla.shape[3]
    q_rank, n_heads, qk_dim = w_uq.shape[1:]
    kv_rank, _, nope = w_uk.shape[1:]
    rope_a = qk_dim - nope
    v_dim = w_uv.shape[3]
    cmp_block = cmp_pe.shape[1]
    kvh, hd = cache_nsa_cmp.shape[4], cache_nsa_cmp.shape[5]
    kv_cols = 2 * kvh * hd
    n_q = 8
    grp = n_q // kvh
    window = 512
    n_exp = w_router.shape[2]
    d_ff = w_e_out.shape[2]
    assert (kvh, hd, rope_a, kv_rank, q_rank, kv_cols) == (2, 64, 32, 128, 256, 256)
    np_, ns_ = b * t, bs * ts
    n = np_ + ns_
    mla_scale = float(qk_dim) ** -0.5
    nsa_scale = float(hd) ** -0.5

    wi = w_in[0]
    o_cq, o_ckv, o_kr, o_qb, o_cmp, o_sel, o_win, o_gate = 0, 256, 384, 416, 928, 1184, 1440, 1696
    misc_w = jnp.concatenate([wi[:, o_kr:o_kr + 32], wi[:, o_gate:o_gate + 24], jnp.zeros((d, 72), F32)], axis=1)
    win_r = jnp.concatenate([wi[:, o_cq:o_cq + 256], wi[:, o_ckv:o_ckv + 128], wi[:, o_qb:o_qb + 512],
                             wi[:, o_cmp:o_cmp + 256], wi[:, o_sel:o_sel + 256], wi[:, o_win:o_win + 256],
                             misc_w], axis=1).astype(BF16)
    wnope = jnp.transpose(w_uq[0][:, :, :nope], (1, 0, 2))
    wuk = jnp.transpose(w_uk[0], (1, 0, 2))
    wrope = jnp.transpose(w_uq[0][:, :, nope:], (1, 0, 2))
    wrope_p = jnp.concatenate([wrope[:, :, rope_a // 2:], wrope[:, :, :rope_a // 2]], axis=2)
    pad_r = lambda w: jnp.transpose(jnp.pad(w, ((0, 0), (0, 0), (0, LANES - rope_a))), (1, 0, 2)).reshape(
        q_rank, n_heads * LANES).astype(BF16)
    wr, wrp = pad_r(wrope), pad_r(wrope_p)
    wuv_t = jnp.transpose(w_uv[0], (1, 0, 2))
    zv = jnp.zeros_like(wuv_t)
    odd = (jnp.arange(n_heads) % 2 == 1)[:, None, None]
    wuv_h = jnp.concatenate([jnp.where(odd, zv, wuv_t), jnp.where(odd, wuv_t, zv)], axis=2).astype(BF16)
    wuv_flat = w_uv[0].reshape(kv_rank, n_heads * v_dim).astype(BF16)

    pe = cmp_pe[0]
    pe_full = jnp.broadcast_to(pe[:, :, None, :], (cmp_block, 2, kvh, hd)).reshape(cmp_block, kv_cols)
    w1 = w_cmp1[0].reshape(2, cmp_block, hd, -1)
    hid = w1.shape[3]
    eye = jnp.eye(kvh, dtype=F32)
    w1bd = jnp.einsum('srdh,kK->srkdKh', w1, eye).reshape(2, cmp_block, kvh * hd, kvh * hid).astype(BF16)
    w2bd = jnp.einsum('shd,kK->skhKd', w_cmp2[0], eye).reshape(2, kvh * hid, kvh * hd).astype(BF16)

    wo = w_o[0].astype(BF16)
    wrt = jnp.pad(w_router[0], ((0, 0), (0, LANES - n_exp)))
    wrh = wrt.astype(BF16)
    wrl = (wrt - wrh.astype(F32)).astype(BF16)
    br = jnp.concatenate([b_router[0], jnp.full((LANES - n_exp,), NEG_INF, F32)])[None, :]

    pos = jnp.concatenate([jnp.tile(jnp.arange(t, dtype=jnp.int32), b),
                           jnp.tile(past + jnp.arange(ts, dtype=jnp.int32), bs)])
    cosa, sina = _rope_tables(pos, rope_a, 1, LANES)
    cosb, sinb = _rope_tables(pos, hd, 2, LANES)

    x = jnp.concatenate([x_prompt.reshape(np_, d), x_sample.reshape(ns_, d)], axis=0)
    tm = _tile_rows(n, 256)
    (rows_a, kmla, qmla, rows_c, rows_s, rows_w, kvs, kvw, qraw, qrope, gates) = _proj(
        x, cosa, sina, cosb, sinb, g_attn, win_r, g_q, g_kv, wnope, wuk, wr, wrp,
        tm=tm, mla_scale=mla_scale, nsa_scale=nsa_scale)

    tq_a = _tile_rows(t, 128)
    tk_a = _tile_rows(t, 512)
    oa_p = _mla_prompt(qmla, kmla, wuv_h, b=b, t=t, tq=tq_a, tk=tk_a)

    pages_c = 16
    cpages = np_ // page_rows
    rows_c_pages = rows_c[:np_].reshape(cpages, page_rows, kv_cols)
    blocks_p = _compress(jnp.arange(cpages, dtype=jnp.int32), rows_c_pages, pe_full, w1bd, w2bd,
                         pages=min(pages_c, cpages))
    tq_b = _tile_rows(t, 256)
    oc_p, selmask = _cmp_prompt(qraw, blocks_p, gates, b=b, t=t, tq=tq_b, cmp_block=cmp_block)
    os_p = _sel_prompt(qrope, kvs, selmask, gates, b=b, t=t, tq=tq_b, tk=_tile_rows(t, 512), cmp_block=cmp_block)
    ow_p = _win_prompt(qrope, kvw, gates, b=b, t=t, tq=tq_b, window=window)

    q_s = jnp.transpose(qmla[:, np_:, :].reshape(n_heads, bs, ts, 256), (1, 0, 2, 3)).reshape(bs, n_heads * ts, 256)
    knew = jnp.pad(kmla[np_:].reshape(bs, ts, 256), ((0, 0), (0, 16 - ts), (0, 0)))
    pages_a = min(16, n_pages)
    oa_s = _mla_sample(page_table, q_s, knew, wuv_flat, cache_mla[0], pages=pages_a, ts=ts).reshape(ns_, -1)

    sum_past = _compress(page_table.reshape(-1), cache_nsa_cmp[0].reshape(-1, page_rows, kv_cols), pe_full, w1bd, w2bd,
                         pages=min(pages_c, n_pages))
    per_page_c = page_rows // cmp_block
    new_len = -(-ts // SEL_BLOCK) * SEL_BLOCK
    new_pages = -(-new_len // page_rows)
    rc_new = jnp.pad(rows_c[np_:].reshape(bs, ts, kv_cols), ((0, 0), (0, new_pages * page_rows - ts), (0, 0)))
    sum_new = _compress(jnp.arange(bs * new_pages, dtype=jnp.int32), rc_new.reshape(bs * new_pages, page_rows, kv_cols),
                        pe_full, w1bd, w2bd, pages=min(pages_c, bs * new_pages))
    nc_s = past // cmp_block + new_len // cmp_block
    nc_pad = -(-nc_s // LANES) * LANES
    blocks_s = jnp.concatenate([
        sum_past.reshape(bs, n_pages * per_page_c, kv_cols),
        sum_new.reshape(bs, new_pages * per_page_c, kv_cols)[:, :new_len // cmp_block],
        jnp.zeros((bs, nc_pad - nc_s, kv_cols), F32)], axis=1)
    to_s = lambda q: jnp.transpose(q[:, np_:, :].reshape(kvh, grp, bs, ts, LANES), (2, 0, 3, 1, 4)).reshape(
        bs, kvh * ts * grp, LANES)
    gates_s = gates[np_:].reshape(bs, ts, LANES)
    n_sel = min(N_SEL, nc_s * cmp_block // SEL_BLOCK)
    oc_s, idx = _cmp_sample(to_s(qraw), blocks_s, gates_s, ts=ts, past=past, cmp_block=cmp_block, n_sel=n_sel, grp=grp)
    idx_flat = idx[:, :, :n_sel].reshape(-1)
    snew = jnp.pad(rows_s[np_:].reshape(bs, ts, kv_cols), ((0, 0), (0, SEL_BLOCK - ts), (0, 0)))
    wnew = jnp.pad(rows_w[np_:].reshape(bs, ts, kv_cols), ((0, 0), (0, 8 - ts), (0, 0)))
    os_s, ow_s = _selwin_sample(idx_flat, page_table, to_s(qrope), state_nsa_win[0].reshape(bs, -1, kv_cols), wnew, gates_s,
                                cache_nsa_sel[0].reshape(-1, page_rows, kv_cols), snew,
                                ts=ts, grp=grp, n_sel=n_sel, past_blocks=past // SEL_BLOCK,
                                per_page=page_rows // SEL_BLOCK)

    cat = lambda p_, s_: jnp.concatenate([p_, s_.reshape(ns_, -1).astype(p_.dtype)], axis=0)
    h1, xn, top_e, gate_w = _outproj(x, cat(oa_p, oa_s), cat(oc_p, oc_s), cat(os_p, os_s), cat(ow_p, ow_s),
                                     wo, g_ffn, wrh, wrl, br, tm=tm, top_k=TOP_K)

    bm = 256
    flat_e = top_e[:, :TOP_K].reshape(-1)
    n_assign = flat_e.shape[0]
    onehot = (flat_e[:, None] == jnp.arange(n_exp, dtype=jnp.int32)[None, :]).astype(jnp.int32)
    rank = jnp.take_along_axis(jnp.cumsum(onehot, axis=0), flat_e[:, None], axis=1)[:, 0] - 1
    counts = jnp.sum(onehot, axis=0)
    padded = (counts + bm - 1) // bm * bm
    pad_end = jnp.cumsum(padded)
    dest = (pad_end - padded)[flat_e] + rank
    n_blocks = -(-n_assign // bm) + n_exp
    row_tok = jnp.zeros((n_blocks * bm,), jnp.int32).at[dest].set(jnp.arange(n_assign, dtype=jnp.int32) // TOP_K)
    row_gate = jnp.zeros((n_blocks * bm,), F32).at[dest].set(gate_w[:, :TOP_K].reshape(-1))
    block_e = jnp.minimum(jnp.searchsorted(pad_end, jnp.arange(n_blocks, dtype=jnp.int32) * bm, side='right'),
                          n_exp - 1).astype(jnp.int32)
    n_used = (pad_end[-1] // bm).astype(jnp.int32)[None]

    x_sorted = _gather_rows(row_tok, xn, rows=bm)
    y_sorted = _experts(block_e, n_used, x_sorted, row_gate[:, None], w_e_in[0], b_e_in[0][:, None, :],
                        w_e_out[0], b_e_out[0][:, None, :], bm=bm)
    y4 = _gather_rows(dest.astype(jnp.int32), y_sorted, rows=_tile_rows(n_assign, 256)).reshape(n, TOP_K * d)

    p_all = jnp.concatenate([p_prompt[0].reshape(np_, -1), p_sample[0].reshape(ns_, -1)], axis=0)
    y = _final(h1, y4, p_all, w_ple[0].astype(BF16), g_ple, w_ple_gate[0].astype(BF16), g_final[None, :],
               tm=tm, top_k=TOP_K)

    kv_shape = (2, kvh, hd)
    wlen = state_nsa_win.shape[2]
    rows_w_p = rows_w[:np_].reshape(b, t, kv_cols)
    win_s = jnp.concatenate([state_nsa_win[0].reshape(bs, wlen, kv_cols), rows_w[np_:].reshape(bs, ts, kv_cols)], axis=1)[:, ts:]
    return (
        y[:np_].reshape(b, t, d), y[np_:].reshape(bs, ts, d),
        rows_a[:np_].reshape(1, b, t, lat), rows_a[np_:].reshape(1, bs, ts, lat),
        rows_c[:np_].reshape((1, b, t) + kv_shape), rows_c[np_:].reshape((1, bs, ts) + kv_shape),
        rows_s[:np_].reshape((1, b, t) + kv_shape), rows_s[np_:].reshape((1, bs, ts) + kv_shape),
        rows_w_p[:, t - min(window, t):].reshape((1, b, min(window, t)) + kv_shape),
        win_s.reshape((1, bs, wlen) + kv_shape),
    )
```

```python
import functools

import jax
import jax.numpy as jnp
from jax import lax
from jax.experimental import pallas as pl
from jax.experimental.pallas import tpu as pltpu

RMS_EPS = 1e-6
ROPE_THETA = 10000.0
NEG_INF = -1e30
REMOVED = -3e38
FORCE_SCORE = 1e4
SEL_BLOCK = 64
N_SEL = 16
TOP_K = 4
SWIGLU_LIMIT = 7.0
SWIGLU_ALPHA = 1.702

LANES = 128
VMEM_LIMIT = 56 * 1024 * 1024

F32 = jnp.float32
BF16 = jnp.bfloat16


def _cparams(sem):
    return pltpu.CompilerParams(dimension_semantics=sem, vmem_limit_bytes=VMEM_LIMIT)


def _nt(a, b):
    return lax.dot_general(a, b, (((1,), (1,)), ((), ())), preferred_element_type=F32)


def _nn(a, b):
    return jnp.dot(a, b, preferred_element_type=F32)


def _split(x):
    hi = x.astype(BF16)
    lo = (x - hi.astype(F32)).astype(BF16)
    return hi, lo


def _lane(shape, dim=None):
    return lax.broadcasted_iota(jnp.int32, shape, len(shape) - 1 if dim is None else dim)


def _partner(x, half):
    w = x.shape[-1]
    lane = _lane(x.shape)
    first = (lane % (2 * half)) < half
    return jnp.where(first, pltpu.roll(x, w - half, x.ndim - 1), pltpu.roll(x, half, x.ndim - 1))


def _swap64(x):
    return pltpu.roll(x, 64, x.ndim - 1)


def _proj_kernel(x_ref, ga_ref, win_ref, gq_ref, gkv_ref, wnope_ref, wuk_ref, wr_ref, wrp_ref,
                 cosa_ref, sina_ref, cosb_ref, sinb_ref,
                 rowsa_ref, kmla_ref, qmla_ref, rowsc_ref, rowss_ref, rowsw_ref,
                 kvs_ref, kvw_ref, qraw_ref, qrope_ref, gates_ref,
                 wabs_ref, *, mla_scale, nsa_scale):
    n_heads = qmla_ref.shape[0]

    @pl.when(pl.program_id(0) == 0)
    def _():
        for h in range(n_heads):
            a_hi, a_lo = _split(wnope_ref[h])
            b_hi, b_lo = _split(wuk_ref[h])
            w = _nt(a_hi, b_hi) + _nt(a_lo, b_hi) + _nt(a_hi, b_lo)
            wabs_ref[:, h * LANES:(h + 1) * LANES] = w.astype(BF16)

    x = x_ref[...]
    xn = x * lax.rsqrt(jnp.mean(x * x, axis=-1, keepdims=True) + RMS_EPS) * ga_ref[...]
    proj = _nn(xn.astype(BF16), win_ref[...])
    cq = proj[:, 0:256]
    ckv = proj[:, 256:384]
    qb = proj[:, 384:896]
    cmp_raw = proj[:, 896:1152]
    sel_raw = proj[:, 1152:1408]
    win_raw = proj[:, 1408:1664]
    misc = proj[:, 1664:1792]

    cosa, sina = cosa_ref[...], sina_ref[...]
    cosb, sinb = cosb_ref[...], sinb_ref[...]

    cqn = (cq * lax.rsqrt(jnp.mean(cq * cq, axis=-1, keepdims=True) + RMS_EPS) * gq_ref[...]).astype(BF16)
    ckvn = ckv * lax.rsqrt(jnp.mean(ckv * ckv, axis=-1, keepdims=True) + RMS_EPS) * gkv_ref[...]
    krope = misc * cosa + _partner(misc, 16) * sina
    rowsa_ref[:, 0:128] = ckvn
    rowsa_ref[:, 128:160] = krope[:, 0:32]
    kmla_ref[...] = jnp.concatenate([ckvn, krope], axis=1).astype(BF16)
    qlat = _nn(cqn, wabs_ref[...])
    qr = _nn(cqn, wr_ref[...])
    qrp = _nn(cqn, wrp_ref[...])
    for h in range(n_heads):
        sl = slice(h * LANES, (h + 1) * LANES)
        rope_h = qr[:, sl] * cosa + qrp[:, sl] * sina
        qmla_ref[h] = (jnp.concatenate([qlat[:, sl], rope_h], axis=1) * mla_scale).astype(BF16)

    rowsc_ref[...] = cmp_raw
    sel_k = sel_raw[:, 0:128] * cosb + _partner(sel_raw[:, 0:128], 32) * sinb
    win_k = win_raw[:, 0:128] * cosb + _partner(win_raw[:, 0:128], 32) * sinb
    rows_s = jnp.concatenate([sel_k, sel_raw[:, 128:256]], axis=1)
    rows_w = jnp.concatenate([win_k, win_raw[:, 128:256]], axis=1)
    rowss_ref[...] = rows_s
    rowsw_ref[...] = rows_w
    kvs_ref[...] = rows_s.astype(BF16)
    kvw_ref[...] = rows_w.astype(BF16)
    cos4 = jnp.concatenate([cosb] * 4, axis=1)
    sin4 = jnp.concatenate([sinb] * 4, axis=1)
    qb_rope = qb * cos4 + _partner(qb, 32) * sin4
    lane = _lane((x.shape[0], LANES))
    n_q = qraw_ref.shape[0]
    for hh in range(n_q):
        kvh = hh // (n_q // 2)
        blk = slice((hh // 2) * LANES, (hh // 2 + 1) * LANES)
        keep = (lane >= 64) if kvh == 1 else (lane < 64)
        for src, dst in ((qb, qraw_ref), (qb_rope, qrope_ref)):
            v = src[:, blk]
            if (hh % 2) != kvh:
                v = _swap64(v)
            dst[hh] = jnp.where(keep, v * nsa_scale, 0.0).astype(BF16)
    gates_ref[...] = jax.nn.sigmoid(misc)


def _proj(x, cosa, sina, cosb, sinb, ga, win_r, gq, gkv, wnope, wuk, wr, wrp, *, tm, mla_scale, nsa_scale):
    n, d = x.shape
    n_heads = wnope.shape[0]
    grid = (n // tm,)
    row = lambda i: (i, 0)
    full2 = lambda i: (0, 0)
    full3 = lambda i: (0, 0, 0)
    head_row = lambda i: (0, i, 0)
    in_specs = [
        pl.BlockSpec((tm, d), row),
        pl.BlockSpec(ga.shape, full2),
        pl.BlockSpec(win_r.shape, full2),
        pl.BlockSpec(gq.shape, full2),
        pl.BlockSpec(gkv.shape, full2),
        pl.BlockSpec(wnope.shape, full3),
        pl.BlockSpec(wuk.shape, full3),
        pl.BlockSpec(wr.shape, full2),
        pl.BlockSpec(wrp.shape, full2),
        pl.BlockSpec((tm, LANES), row), pl.BlockSpec((tm, LANES), row),
        pl.BlockSpec((tm, LANES), row), pl.BlockSpec((tm, LANES), row),
    ]
    out_shape = [
        jax.ShapeDtypeStruct((n, 160), F32),
        jax.ShapeDtypeStruct((n, 256), BF16),
        jax.ShapeDtypeStruct((n_heads, n, 256), BF16),
        jax.ShapeDtypeStruct((n, 256), F32),
        jax.ShapeDtypeStruct((n, 256), F32),
        jax.ShapeDtypeStruct((n, 256), F32),
        jax.ShapeDtypeStruct((n, 256), BF16),
        jax.ShapeDtypeStruct((n, 256), BF16),
        jax.ShapeDtypeStruct((8, n, LANES), BF16),
        jax.ShapeDtypeStruct((8, n, LANES), BF16),
        jax.ShapeDtypeStruct((n, LANES), F32),
    ]
    out_specs = [
        pl.BlockSpec((tm, 160), row),
        pl.BlockSpec((tm, 256), row),
        pl.BlockSpec((n_heads, tm, 256), head_row),
        pl.BlockSpec((tm, 256), row),
        pl.BlockSpec((tm, 256), row),
        pl.BlockSpec((tm, 256), row),
        pl.BlockSpec((tm, 256), row),
        pl.BlockSpec((tm, 256), row),
        pl.BlockSpec((8, tm, LANES), head_row),
        pl.BlockSpec((8, tm, LANES), head_row),
        pl.BlockSpec((tm, LANES), row),
    ]
    return pl.pallas_call(
        functools.partial(_proj_kernel, mla_scale=mla_scale, nsa_scale=nsa_scale), name="proj",
        grid=grid, in_specs=in_specs, out_specs=out_specs, out_shape=out_shape,
        scratch_shapes=[pltpu.VMEM((wnope.shape[1], n_heads * LANES), BF16)],
        compiler_params=_cparams(("arbitrary",)),
    )(x, ga, win_r, gq, gkv, wnope, wuk, wr, wrp, cosa, sina, cosb, sinb)


def _mla_prompt_kernel(q_ref, k_ref, wuv_ref, o_ref, m_scr, l_scr, acc_scr, *, tq, tk):
    i, j = pl.program_id(1), pl.program_id(2)
    nh = q_ref.shape[0]
    jmax = (i * tq + tq - 1) // tk

    @pl.when(j == 0)
    def _():
        m_scr[...] = jnp.full(m_scr.shape, NEG_INF, F32)
        l_scr[...] = jnp.zeros(l_scr.shape, F32)
        acc_scr[...] = jnp.zeros(acc_scr.shape, F32)

    def step(masked):
        q = q_ref[...].reshape(nh * tq, q_ref.shape[2])
        k = k_ref[...]
        s = _nt(q, k)
        if masked:
            qpos = i * tq + lax.broadcasted_iota(jnp.int32, (1, tq, tk), 1)
            kpos = j * tk + lax.broadcasted_iota(jnp.int32, (1, tq, tk), 2)
            s = jnp.where(kpos <= qpos, s.reshape(nh, tq, tk), NEG_INF).reshape(nh * tq, tk)
        m_old = m_scr[...]
        m_new = jnp.maximum(m_old, jnp.max(s, axis=-1, keepdims=True))
        alpha = jnp.exp(m_old - m_new)
        p = jnp.exp(s - m_new)
        l_scr[...] = l_scr[...] * alpha + jnp.sum(p, axis=-1, keepdims=True)
        acc_scr[...] = acc_scr[...] * alpha + _nn(p.astype(BF16), k[:, 0:LANES])
        m_scr[...] = m_new

    full = (j + 1) * tk - 1 <= i * tq
    pl.when(full)(lambda: step(False))
    pl.when(jnp.logical_and(j <= jmax, jnp.logical_not(full)))(lambda: step(True))

    @pl.when(j == pl.num_programs(2) - 1)
    def _():
        o = (acc_scr[...] / l_scr[...]).astype(BF16).reshape(nh, tq, LANES)
        for m in range(nh // 2):
            pair = _nn(o[2 * m], wuv_ref[2 * m]) + _nn(o[2 * m + 1], wuv_ref[2 * m + 1])
            o_ref[:, m * LANES:(m + 1) * LANES] = pair.astype(o_ref.dtype)


def _mla_prompt(qmla, kmla, wuv, *, b, t, tq, tk):
    nh = qmla.shape[0]
    nq, nk = t // tq, t // tk
    out_w = nh * wuv.shape[2] // 2

    def kmap(bb, i, j):
        return (bb * nk + jnp.minimum(j, (i * tq + tq - 1) // tk), 0)

    return pl.pallas_call(
        functools.partial(_mla_prompt_kernel, tq=tq, tk=tk), name="mla_prompt",
        grid=(b, nq, nk),
        in_specs=[
            pl.BlockSpec((nh, tq, qmla.shape[2]), lambda bb, i, j: (0, bb * nq + i, 0)),
            pl.BlockSpec((tk, kmla.shape[1]), kmap),
            pl.BlockSpec(wuv.shape, lambda bb, i, j: (0, 0, 0)),
        ],
        out_specs=pl.BlockSpec((tq, out_w), lambda bb, i, j: (bb * nq + i, 0)),
        out_shape=jax.ShapeDtypeStruct((b * t, out_w), BF16),
        scratch_shapes=[pltpu.VMEM((nh * tq, 1), F32), pltpu.VMEM((nh * tq, 1), F32),
                        pltpu.VMEM((nh * tq, LANES), F32)],
        compiler_params=_cparams(("arbitrary", "arbitrary", "arbitrary")),
    )(qmla, kmla, wuv)


def _compress_kernel(tab_ref, src_ref, pe_ref, w1_ref, w2_ref, o_ref, buf, sem, *xbuf,
                     pages, page_rows, cmp_block, transposed):
    g = pl.program_id(0)
    n = pl.num_programs(0)
    slot = g % 2

    def copies(step, sl):
        if transposed:
            return [pltpu.make_async_copy(src_ref.at[tab_ref[step * pages + p]], buf.at[sl, p], sem.at[sl])
                    for p in range(pages)]
        return [pltpu.make_async_copy(src_ref.at[tab_ref[step * pages + p], :, pl.ds(s * LANES, LANES)],
                                      buf.at[sl, s, pl.ds(p * page_rows, page_rows), :], sem.at[sl])
                for p in range(pages) for s in range(2)]

    @pl.when(g == 0)
    def _():
        for c in copies(0, 0):
            c.start()

    @pl.when(g + 1 < n)
    def _():
        for c in copies(g + 1, 1 - slot):
            c.start()

    for c in copies(g, slot):
        c.wait()

    if transposed:
        rows_ref = xbuf[0]

        def to_rows(p, c):
            for s in range(2):
                blk = buf[slot, p, s * LANES:(s + 1) * LANES, :]
                rows_ref[s, pl.ds(pl.multiple_of(p * page_rows, page_rows), page_rows), :] = blk.T
            return c

        lax.fori_loop(0, pages, to_rows, 0)
        load = lambda s, r, m: rows_ref[s, pl.ds(r, m, stride=cmp_block), :]
    else:
        load = lambda s, r, m: buf[slot, s, pl.ds(r, m, stride=cmp_block), :]

    nblk = pages * page_rows // cmp_block
    for s in range(2):
        acc = jnp.zeros((nblk, w1_ref.shape[3]), F32)
        for r in range(cmp_block):
            xr = load(s, r, nblk) + pe_ref[r:r + 1, s * LANES:(s + 1) * LANES]
            acc = acc + _nn(xr.astype(BF16), w1_ref[s, r])
        hid = jax.nn.gelu(acc)
        o_ref[:, s * LANES:(s + 1) * LANES] = _nn(hid.astype(BF16), w2_ref[s])


def _compress(table, src_pages, pe_full, w1bd, w2bd, *, pages, transposed):
    n_pages = table.shape[0]
    cmp_block, width = pe_full.shape
    page_rows = src_pages.shape[2] if transposed else src_pages.shape[1]
    per_page = page_rows // cmp_block
    if transposed:
        assert page_rows == LANES
        scratch = [pltpu.VMEM((2, pages, width, page_rows), F32), pltpu.SemaphoreType.DMA((2,)),
                   pltpu.VMEM((2, pages * page_rows, LANES), F32)]
    else:
        scratch = [pltpu.VMEM((2, 2, pages * page_rows, LANES), F32), pltpu.SemaphoreType.DMA((2,))]
    grid_spec = pltpu.PrefetchScalarGridSpec(
        num_scalar_prefetch=1, grid=(n_pages // pages,),
        in_specs=[
            pl.BlockSpec(memory_space=pl.ANY),
            pl.BlockSpec(pe_full.shape, lambda g, tab: (0, 0)),
            pl.BlockSpec(w1bd.shape, lambda g, tab: (0, 0, 0, 0)),
            pl.BlockSpec(w2bd.shape, lambda g, tab: (0, 0, 0)),
        ],
        out_specs=pl.BlockSpec((pages * per_page, width), lambda g, tab: (g, 0)),
        scratch_shapes=scratch,
    )
    return pl.pallas_call(
        functools.partial(_compress_kernel, pages=pages, page_rows=page_rows, cmp_block=cmp_block,
                          transposed=transposed),
        name="compress_t" if transposed else "compress",
        grid_spec=grid_spec,
        out_shape=jax.ShapeDtypeStruct((n_pages * per_page, width), F32),
        compiler_params=_cparams(("arbitrary",)),
    )(table, src_pages, pe_full, w1bd, w2bd)


def _cmp_softmax(q, kc, vc, valid):
    s = jnp.where(valid, _nt(q, kc), NEG_INF)
    m = jnp.max(s, axis=-1, keepdims=True)
    e = jnp.where(valid, jnp.exp(s - m), 0.0)
    l = jnp.sum(e, axis=-1, keepdims=True)
    p = e / jnp.where(l == 0.0, 1.0, l)
    return p, _nn(p.astype(BF16), vc)


def _select(imp, pos, n_sel, cmp_per_sel, cmp_block):
    nc = imp.shape[-1]
    lane = _lane(imp.shape)
    pair = imp
    for d in range(1, cmp_per_sel):
        pair = pair + pltpu.roll(imp, nc - d, imp.ndim - 1)
    j = lane // cmp_per_sel
    qb = pos // (cmp_per_sel * cmp_block)
    forced = (j == 0) | (j == qb) | (j == qb - 1)
    causal = j * (cmp_per_sel * cmp_block) <= pos
    score = jnp.where(causal, jnp.where(forced, FORCE_SCORE, pair), NEG_INF)
    score = jnp.where(lane % cmp_per_sel == 0, score, REMOVED)
    lane_f = lane.astype(F32)
    picks = []
    for _ in range(n_sel):
        m = jnp.max(score, axis=-1, keepdims=True)
        first = jnp.min(jnp.where(score == m, lane_f, 1e9), axis=-1, keepdims=True)
        hit = lane_f == first
        score = jnp.where(hit, REMOVED, score)
        picks.append(first)
    return picks


def _pair_heads(o_even, o_odd, kvh):
    lane = _lane(o_even.shape)
    a = o_even if kvh == 0 else _swap64(o_even)
    b = _swap64(o_odd) if kvh == 0 else o_odd
    return jnp.where(lane < 64, a, b)


def _gate_col(gates, hh, c):
    col = 32 + hh * 3 + c
    return gates[:, col:col + 1]


def _cmp_prompt_kernel(q_ref, bc_ref, g_ref, o_ref, sel_ref, *, tq, cmp_block, n_sel):
    i = pl.program_id(1)
    nc = bc_ref.shape[0]
    gates = g_ref[...]
    bc = bc_ref[...]
    kc = bc[:, 0:LANES].astype(BF16)
    vc = bc[:, LANES:2 * LANES].astype(BF16)
    pos = i * tq + lax.broadcasted_iota(jnp.int32, (tq, 1), 0)
    lane = _lane((tq, nc))
    valid = lane * cmp_block + (cmp_block - 1) <= pos
    cmp_per_sel = SEL_BLOCK // cmp_block
    n_q = q_ref.shape[0]
    grp = n_q // 2
    for kvh in range(2):
        imp = jnp.zeros((tq, nc), F32)
        outs = []
        for g in range(grp):
            hh = kvh * grp + g
            p, o = _cmp_softmax(q_ref[hh], kc, vc, valid)
            imp = imp + p
            outs.append(o * _gate_col(gates, hh, 0))
        for g2 in range(grp // 2):
            blk = kvh * (grp // 2) + g2
            o_ref[:, blk * LANES:(blk + 1) * LANES] = _pair_heads(outs[2 * g2], outs[2 * g2 + 1], kvh)
        picks = _select(imp, pos, n_sel, cmp_per_sel, cmp_block)
        lane_f = lane.astype(F32)
        selm = jnp.zeros((tq, nc), F32)
        for first in picks:
            selm = jnp.where(lane_f == first, 1.0, selm)
        sel_ref[kvh] = selm.astype(BF16)


def _cmp_prompt(qraw, blocks_c, gates, *, b, t, tq, cmp_block):
    n_q = qraw.shape[0]
    nq = t // tq
    nc = t // cmp_block
    n_sel = min(N_SEL, t // SEL_BLOCK)
    return pl.pallas_call(
        functools.partial(_cmp_prompt_kernel, tq=tq, cmp_block=cmp_block, n_sel=n_sel), name="cmp_prompt",
        grid=(b, nq),
        in_specs=[
            pl.BlockSpec((n_q, tq, LANES), lambda bb, i: (0, bb * nq + i, 0)),
            pl.BlockSpec((nc, blocks_c.shape[1]), lambda bb, i: (bb, 0)),
            pl.BlockSpec((tq, LANES), lambda bb, i: (bb * nq + i, 0)),
        ],
        out_specs=[
            pl.BlockSpec((tq, n_q * 64), lambda bb, i: (bb * nq + i, 0)),
            pl.BlockSpec((2, tq, nc), lambda bb, i: (0, bb * nq + i, 0)),
        ],
        out_shape=[jax.ShapeDtypeStruct((b * t, n_q * 64), F32),
                   jax.ShapeDtypeStruct((2, b * t, nc), BF16)],
        compiler_params=_cparams(("arbitrary", "arbitrary")),
    )(qraw, blocks_c, gates)


def _sel_prompt_kernel(q_ref, k_ref, v_ref, sel_ref, g_ref, o_ref, m_scr, l_scr, acc_scr, *, tq, tk, cmp_block):
    kvh, i, j = pl.program_id(1), pl.program_id(2), pl.program_id(3)
    grp = q_ref.shape[0]
    nc = sel_ref.shape[2]
    jmax = (i * tq + tq - 1) // tk
    cmp_per_sel = SEL_BLOCK // cmp_block

    @pl.when(j == 0)
    def _():
        m_scr[...] = jnp.full(m_scr.shape, NEG_INF, F32)
        l_scr[...] = jnp.zeros(l_scr.shape, F32)
        acc_scr[...] = jnp.zeros(acc_scr.shape, F32)

    @pl.when(j <= jmax)
    def _():
        q = q_ref[...].reshape(grp * tq, LANES)
        s = _nt(q, k_ref[...]).reshape(grp, tq, tk)
        row_l = lax.broadcasted_iota(jnp.int32, (nc, tk), 0)
        key = j * tk + lax.broadcasted_iota(jnp.int32, (nc, tk), 1)
        expand = (row_l == (key // SEL_BLOCK) * cmp_per_sel).astype(BF16)
        msel = _nn(sel_ref[0], expand)
        qpos = i * tq + lax.broadcasted_iota(jnp.int32, (tq, tk), 0)
        kpos = j * tk + lax.broadcasted_iota(jnp.int32, (tq, tk), 1)
        valid = ((msel > 0.5) & (kpos <= qpos))[None]
        s = jnp.where(valid, s, NEG_INF)
        m_old = m_scr[...].reshape(grp, tq, 1)
        m_new = jnp.maximum(m_old, jnp.max(s, axis=-1, keepdims=True))
        alpha = jnp.exp(m_old - m_new)
        p = jnp.where(valid, jnp.exp(s - m_new), 0.0)
        l_scr[...] = (l_scr[...].reshape(grp, tq, 1) * alpha + jnp.sum(p, axis=-1, keepdims=True)).reshape(grp * tq, 1)
        pv = _nn(p.reshape(grp * tq, tk).astype(BF16), v_ref[...])
        acc_scr[...] = acc_scr[...] * alpha.reshape(grp * tq, 1) + pv
        m_scr[...] = m_new.reshape(grp * tq, 1)

    @pl.when(j == pl.num_programs(3) - 1)
    def _():
        l = l_scr[...]
        o = (acc_scr[...] / jnp.where(l == 0.0, 1.0, l)).reshape(grp, tq, LANES)
        gates = g_ref[...]
        for g2 in range(grp // 2):
            for kk in range(2):
                @pl.when(kvh == kk)
                def _():
                    a = o[2 * g2] * _gate_col(gates, kk * grp + 2 * g2, 1)
                    b2 = o[2 * g2 + 1] * _gate_col(gates, kk * grp + 2 * g2 + 1, 1)
                    o_ref[:, g2 * LANES:(g2 + 1) * LANES] = _pair_heads(a, b2, kk)


def _sel_prompt(qrope, kvs, selmask, gates, *, b, t, tq, tk, cmp_block):
    n_q = qrope.shape[0]
    grp = n_q // 2
    nq, nk = t // tq, t // tk
    nc = selmask.shape[2]

    def kv_map(lane_blk):
        def f(bb, kvh, i, j):
            return (bb * nk + jnp.minimum(j, (i * tq + tq - 1) // tk), lane_blk)
        return f

    return pl.pallas_call(
        functools.partial(_sel_prompt_kernel, tq=tq, tk=tk, cmp_block=cmp_block), name="sel_prompt",
        grid=(b, 2, nq, nk),
        in_specs=[
            pl.BlockSpec((grp, tq, LANES), lambda bb, kvh, i, j: (kvh, bb * nq + i, 0)),
            pl.BlockSpec((tk, LANES), kv_map(0)),
            pl.BlockSpec((tk, LANES), kv_map(1)),
            pl.BlockSpec((1, tq, nc), lambda bb, kvh, i, j: (kvh, bb * nq + i, 0)),
            pl.BlockSpec((tq, LANES), lambda bb, kvh, i, j: (bb * nq + i, 0)),
        ],
        out_specs=pl.BlockSpec((tq, grp * 64), lambda bb, kvh, i, j: (bb * nq + i, kvh)),
        out_shape=jax.ShapeDtypeStruct((b * t, n_q * 64), F32),
        scratch_shapes=[pltpu.VMEM((grp * tq, 1), F32), pltpu.VMEM((grp * tq, 1), F32),
                        pltpu.VMEM((grp * tq, LANES), F32)],
        compiler_params=_cparams(("arbitrary",) * 4),
    )(qrope, kvs, kvs, selmask, gates)


def _win_prompt_kernel(q_ref, k0, k1, k2, v0, v1, v2, g_ref, o_ref, *, tq, window):
    kvh, i = pl.program_id(1), pl.program_id(2)
    grp = q_ref.shape[0]
    q = q_ref[...].reshape(grp * tq, LANES)
    qpos = i * tq + lax.broadcasted_iota(jnp.int32, (tq, tq), 0)
    col = lax.broadcasted_iota(jnp.int32, (tq, tq), 1)
    ss, vs = [], []
    for d, (kr, vr) in enumerate(((k0, v0), (k1, v1), (k2, v2))):
        blk = i - 2 + d
        kpos = blk * tq + col
        valid = ((kpos >= 0) & (kpos <= qpos) & (kpos > qpos - window))[None]
        s = _nt(q, kr[...]).reshape(grp, tq, tq)
        ss.append(jnp.where(valid, s, NEG_INF))
        vs.append(valid)
    s = jnp.concatenate(ss, axis=2)
    valid = jnp.concatenate(vs, axis=2)
    m = jnp.max(s, axis=-1, keepdims=True)
    e = jnp.where(valid, jnp.exp(s - m), 0.0)
    l = jnp.sum(e, axis=-1, keepdims=True)
    p = (e / jnp.where(l == 0.0, 1.0, l)).reshape(grp * tq, 3 * tq).astype(BF16)
    o = _nn(p[:, 0:tq], v0[...]) + _nn(p[:, tq:2 * tq], v1[...]) + _nn(p[:, 2 * tq:3 * tq], v2[...])
    o = o.reshape(grp, tq, LANES)
    gates = g_ref[...]
    for g2 in range(grp // 2):
        for kk in range(2):
            @pl.when(kvh == kk)
            def _():
                a = o[2 * g2] * _gate_col(gates, kk * grp + 2 * g2, 2)
                b2 = o[2 * g2 + 1] * _gate_col(gates, kk * grp + 2 * g2 + 1, 2)
                o_ref[:, g2 * LANES:(g2 + 1) * LANES] = _pair_heads(a, b2, kk)


def _win_prompt(qrope, kvw, gates, *, b, t, tq, window):
    assert window == 2 * tq
    n_q = qrope.shape[0]
    grp = n_q // 2
    nq = t // tq

    def kv_map(d, lane_blk):
        def f(bb, kvh, i):
            return (bb * nq + jnp.maximum(i - 2 + d, 0), lane_blk)
        return f

    kv_specs = [pl.BlockSpec((tq, LANES), kv_map(d, lb)) for lb in (0, 1) for d in range(3)]
    return pl.pallas_call(
        functools.partial(_win_prompt_kernel, tq=tq, window=window), name="win_prompt",
        grid=(b, 2, nq),
        in_specs=[pl.BlockSpec((grp, tq, LANES), lambda bb, kvh, i: (kvh, bb * nq + i, 0))] + kv_specs
                 + [pl.BlockSpec((tq, LANES), lambda bb, kvh, i: (bb * nq + i, 0))],
        out_specs=pl.BlockSpec((tq, grp * 64), lambda bb, kvh, i: (bb * nq + i, kvh)),
        out_shape=jax.ShapeDtypeStruct((b * t, n_q * 64), F32),
        compiler_params=_cparams(("arbitrary",) * 3),
    )(qrope, kvw, kvw, kvw, kvw, kvw, kvw, gates)


def _mla_sample_kernel(pt_ref, q_ref, knew_ref, wuv_ref, cache_ref, o_ref, buf, sem, *,
                       pages, n_chunks, page_rows, ts, lat):
    b = pl.program_id(0)
    nb = pl.num_programs(0)
    rows = q_ref.shape[1]

    def copies(step, sl):
        return [pltpu.make_async_copy(cache_ref.at[pt_ref[step * pages + p]],
                                      buf.at[sl, :, pl.ds(p * page_rows, page_rows)], sem.at[sl])
                for p in range(pages)]

    @pl.when(b == 0)
    def _():
        for c in copies(0, 0):
            c.start()

    q = q_ref[0]
    q_lat, q_rope = q[:, 0:LANES], q[:, LANES:lat]

    def chunk(c, carry):
        m_old, l_old, acc = carry
        step = b * n_chunks + c
        slot = step % 2

        @pl.when(step + 1 < nb * n_chunks)
        def _():
            for cp in copies(step + 1, 1 - slot):
                cp.start()

        for cp in copies(step, slot):
            cp.wait()
        kb = buf[slot].astype(BF16)
        s = _nn(q_lat, kb[0:LANES, :]) + _nn(q_rope, kb[LANES:lat, :])
        m_new = jnp.maximum(m_old, jnp.max(s, axis=-1, keepdims=True))
        alpha = jnp.exp(m_old - m_new)
        p = jnp.exp(s - m_new)
        l_new = l_old * alpha + jnp.sum(p, axis=-1, keepdims=True)
        acc = acc * alpha + _nt(p.astype(BF16), kb[0:LANES, :])
        return m_new, l_new, acc

    init = (jnp.full((rows, 1), NEG_INF, F32), jnp.zeros((rows, 1), F32), jnp.zeros((rows, LANES), F32))
    m_old, l_old, acc = lax.fori_loop(0, n_chunks, chunk, init)

    kn = knew_ref[0]
    s = _nt(q, kn)
    tok = lax.broadcasted_iota(jnp.int32, s.shape, 0) % ts
    col = lax.broadcasted_iota(jnp.int32, s.shape, 1)
    valid = (col <= tok) & (col < ts)
    s = jnp.where(valid, s, NEG_INF)
    m_new = jnp.maximum(m_old, jnp.max(s, axis=-1, keepdims=True))
    alpha = jnp.exp(m_old - m_new)
    p = jnp.where(valid, jnp.exp(s - m_new), 0.0)
    l_new = l_old * alpha + jnp.sum(p, axis=-1, keepdims=True)
    acc = acc * alpha + _nn(p.astype(BF16), kn[:, 0:LANES])
    o = (acc / l_new).astype(BF16)
    full = _nn(o, wuv_ref[...])
    nh = rows // ts
    vw = wuv_ref.shape[1] // nh
    lane_head = _lane((ts, full.shape[1])) // vw
    out = jnp.zeros((ts, full.shape[1]), F32)
    for h in range(nh):
        out = jnp.where(lane_head == h, full[h * ts:(h + 1) * ts, :], out)
    o_ref[0] = out


def _mla_sample(page_table, q_s, knew, wuv_flat, cache, *, pages, ts):
    bs, n_pages = page_table.shape
    lat, page_rows = cache.shape[1], cache.shape[2]
    n_chunks = n_pages // pages
    rows = q_s.shape[1]
    width = q_s.shape[2]
    grid_spec = pltpu.PrefetchScalarGridSpec(
        num_scalar_prefetch=1, grid=(bs,),
        in_specs=[
            pl.BlockSpec((1, rows, width), lambda b, pt: (b, 0, 0)),
            pl.BlockSpec((1,) + knew.shape[1:], lambda b, pt: (b, 0, 0)),
            pl.BlockSpec(wuv_flat.shape, lambda b, pt: (0, 0)),
            pl.BlockSpec(memory_space=pl.ANY),
        ],
        out_specs=pl.BlockSpec((1, ts, wuv_flat.shape[1]), lambda b, pt: (b, 0, 0)),
        scratch_shapes=[pltpu.VMEM((2, lat, pages * page_rows), F32), pltpu.SemaphoreType.DMA((2,))],
    )
    return pl.pallas_call(
        functools.partial(_mla_sample_kernel, pages=pages, n_chunks=n_chunks, page_rows=page_rows, ts=ts, lat=lat),
        name="mla_sample",
        grid_spec=grid_spec,
        out_shape=jax.ShapeDtypeStruct((bs, ts, wuv_flat.shape[1]), F32),
        compiler_params=_cparams(("arbitrary",)),
    )(page_table.reshape(-1), q_s, knew, wuv_flat, cache)


def _cmp_sample_kernel(q_ref, bc_ref, g_ref, o_ref, idx_ref, *, ts, past, cmp_block, n_sel, grp):
    nb = q_ref.shape[0]
    nc = bc_ref.shape[1]
    rows = ts * grp
    cmp_per_sel = SEL_BLOCK // cmp_block
    lane = _lane((rows, nc))
    pos_r = past + lax.broadcasted_iota(jnp.int32, (rows, 1), 0) // grp
    valid = lane * cmp_block + (cmp_block - 1) <= pos_r
    imps = []
    for bb in range(nb):
        bc = bc_ref[bb]
        kc = bc[:, 0:LANES].astype(BF16)
        vc = bc[:, LANES:2 * LANES].astype(BF16)
        q = q_ref[bb]
        for kvh in range(2):
            p, o = _cmp_softmax(q[kvh * rows:(kvh + 1) * rows], kc, vc, valid)
            imps.append(jnp.sum(p.reshape(ts, grp, nc), axis=1))
            o_k = o if kvh == 0 else _swap64(o)
            o_ref[bb, kvh * rows:(kvh + 1) * rows, :] = o_k[:, 0:64] * g_ref[bb, kvh * rows:(kvh + 1) * rows, 0:1]
    imp = jnp.concatenate(imps, axis=0)
    n_rows = imp.shape[0]
    pos_t = past + lax.broadcasted_iota(jnp.int32, (n_rows, 1), 0) % ts
    picks = _select(imp, pos_t, n_sel, cmp_per_sel, cmp_block)
    lane_o = _lane((n_rows, LANES))
    idx = jnp.zeros((n_rows, LANES), jnp.int32)
    for it, first in enumerate(picks):
        idx = jnp.where(lane_o == it, first.astype(jnp.int32) // cmp_per_sel, idx)
    idx_ref[...] = idx.reshape(nb, 2 * ts, LANES)


def _cmp_sample(q_s, blocks_c, grow, *, ts, past, cmp_block, n_sel, grp, nb):
    bs, qrows = q_s.shape[0], q_s.shape[1]
    return pl.pallas_call(
        functools.partial(_cmp_sample_kernel, ts=ts, past=past, cmp_block=cmp_block, n_sel=n_sel, grp=grp),
        name="cmp_sample",
        grid=(bs // nb,),
        in_specs=[
            pl.BlockSpec((nb,) + q_s.shape[1:], lambda b: (b, 0, 0)),
            pl.BlockSpec((nb,) + blocks_c.shape[1:], lambda b: (b, 0, 0)),
            pl.BlockSpec((nb,) + grow.shape[1:], lambda b: (b, 0, 0)),
        ],
        out_specs=[
            pl.BlockSpec((nb, qrows, 64), lambda b: (b, 0, 0)),
            pl.BlockSpec((nb, 2 * ts, LANES), lambda b: (b, 0, 0)),
        ],
        out_shape=[jax.ShapeDtypeStruct((bs, qrows, 64), F32),
                   jax.ShapeDtypeStruct((bs, 2 * ts, LANES), jnp.int32)],
        compiler_params=_cparams(("arbitrary",)),
    )(q_s, blocks_c, grow)


def _selwin_sample_kernel(idx_ref, pt_ref, q_ref, win_ref, wnew_ref, g_ref, cache_ref, snew_ref,
                          osel_ref, owin_ref, buf, sem, *,
                          ts, grp, n_sel, past_blocks, per_page, n_pages):
    b = pl.program_id(0)
    nb = pl.num_programs(0)
    slot = b % 2

    def issue(bb, sl, wait):
        for kt in range(2 * ts):
            kvh = kt // ts
            for it in range(n_sel):
                blk = idx_ref[(bb * 2 * ts + kt) * n_sel + it]
                for s in range(2):
                    rows_s = pl.ds(s * LANES + kvh * 64, 64)
                    dst = buf.at[sl, kt, s, :, pl.ds(it * LANES, LANES)]
                    if wait:
                        pltpu.make_async_copy(snew_ref.at[bb, rows_s, :], dst, sem.at[sl]).wait()
                        continue
                    in_past = blk < past_blocks

                    @pl.when(in_past)
                    def _():
                        phys = pt_ref[bb * n_pages + blk // per_page]
                        pltpu.make_async_copy(cache_ref.at[phys, rows_s, :], dst, sem.at[sl]).start()

                    @pl.when(jnp.logical_not(in_past))
                    def _():
                        pltpu.make_async_copy(snew_ref.at[bb, rows_s, :], dst, sem.at[sl]).start()

    @pl.when(b == 0)
    def _():
        issue(0, 0, False)

    @pl.when(b + 1 < nb)
    def _():
        issue(b + 1, 1 - slot, False)

    issue(b, slot, True)

    q = q_ref[0]
    gates = g_ref[0]
    nkeys = n_sel * LANES
    lane_k = _lane((1, nkeys))
    within = lane_k % LANES
    rows = ts * grp
    for kvh in range(2):
        for t in range(ts):
            kt = kvh * ts + t
            kk = buf[slot, kt, 0].astype(BF16)
            vv = buf[slot, kt, 1].astype(BF16)
            lo = jnp.zeros((1, nkeys), jnp.int32)
            hi = jnp.zeros((1, nkeys), jnp.int32)
            for it in range(n_sel):
                blk = idx_ref[(b * 2 * ts + kt) * n_sel + it]
                in_past = blk < past_blocks
                half = (blk % per_page) * SEL_BLOCK
                here = (lane_k // LANES) == it
                lo = jnp.where(here, jnp.where(in_past, half, 0), lo)
                hi = jnp.where(here, jnp.where(in_past, half + SEL_BLOCK - 1, t), hi)
            kvalid = (within >= lo) & (within <= hi)
            r0 = kt * grp
            s = jnp.where(kvalid, _nn(q[r0:r0 + grp], kk), NEG_INF)
            m = jnp.max(s, axis=-1, keepdims=True)
            e = jnp.where(kvalid, jnp.exp(s - m), 0.0)
            l = jnp.sum(e, axis=-1, keepdims=True)
            p = e / jnp.where(l == 0.0, 1.0, l)
            osel_ref[0, r0:r0 + grp, :] = _nt(p.astype(BF16), vv) * gates[r0:r0 + grp, 1:2]

        r1 = kvh * rows
        wk = win_ref[0, kvh * 64:(kvh + 1) * 64, :].astype(BF16)
        wv = win_ref[0, LANES + kvh * 64:LANES + (kvh + 1) * 64, :].astype(BF16)
        nk_ = wnew_ref[0, kvh * 64:(kvh + 1) * 64, :].astype(BF16)
        nv_ = wnew_ref[0, LANES + kvh * 64:LANES + (kvh + 1) * 64, :].astype(BF16)
        wlen = wk.shape[1]
        qk = q[r1:r1 + rows]
        tok = lax.broadcasted_iota(jnp.int32, (rows, 1), 0) // grp
        v_old = _lane((rows, wlen)) > tok
        col_n = _lane((rows, LANES))
        v_new = (col_n <= tok) & (col_n < ts)
        s_old = jnp.where(v_old, _nn(qk, wk), NEG_INF)
        s_new = jnp.where(v_new, _nn(qk, nk_), NEG_INF)
        m = jnp.maximum(jnp.max(s_old, axis=-1, keepdims=True), jnp.max(s_new, axis=-1, keepdims=True))
        e_old = jnp.where(v_old, jnp.exp(s_old - m), 0.0)
        e_new = jnp.where(v_new, jnp.exp(s_new - m), 0.0)
        l = jnp.sum(e_old, axis=-1, keepdims=True) + jnp.sum(e_new, axis=-1, keepdims=True)
        inv = 1.0 / jnp.where(l == 0.0, 1.0, l)
        ow = _nt((e_old * inv).astype(BF16), wv) + _nt((e_new * inv).astype(BF16), nv_)
        owin_ref[0, r1:r1 + rows, :] = ow * gates[r1:r1 + rows, 2:3]


def _selwin_sample(idx_flat, page_table, q_c, win_t, wnew_t, grow, cache_sel_t, snew_t, *,
                   ts, grp, n_sel, past_blocks, per_page):
    bs, n_pages = page_table.shape
    qrows = q_c.shape[1]
    grid_spec = pltpu.PrefetchScalarGridSpec(
        num_scalar_prefetch=2, grid=(bs,),
        in_specs=[
            pl.BlockSpec((1,) + q_c.shape[1:], lambda b, i, p: (b, 0, 0)),
            pl.BlockSpec((1,) + win_t.shape[1:], lambda b, i, p: (b, 0, 0)),
            pl.BlockSpec((1,) + wnew_t.shape[1:], lambda b, i, p: (b, 0, 0)),
            pl.BlockSpec((1,) + grow.shape[1:], lambda b, i, p: (b, 0, 0)),
            pl.BlockSpec(memory_space=pl.ANY),
            pl.BlockSpec(memory_space=pl.ANY),
        ],
        out_specs=[pl.BlockSpec((1, qrows, 64), lambda b, i, p: (b, 0, 0)),
                   pl.BlockSpec((1, qrows, 64), lambda b, i, p: (b, 0, 0))],
        scratch_shapes=[pltpu.VMEM((2, 2 * ts, 2, 64, n_sel * LANES), F32), pltpu.SemaphoreType.DMA((2,))],
    )
    return pl.pallas_call(
        functools.partial(_selwin_sample_kernel, ts=ts, grp=grp, n_sel=n_sel, past_blocks=past_blocks,
                          per_page=per_page, n_pages=n_pages),
        name="selwin_sample", grid_spec=grid_spec,
        out_shape=[jax.ShapeDtypeStruct((bs, qrows, 64), F32)] * 2,
        compiler_params=_cparams(("arbitrary",)),
    )(idx_flat, page_table.reshape(-1), q_c, win_t, wnew_t, grow, cache_sel_t, snew_t)


def _outproj_kernel(h_ref, oa_ref, oc_ref, os_ref, ow_ref, wo_ref, gf_ref, wrh_ref, wrl_ref, br_ref,
                    h1_ref, xn_ref, e_ref, gw_ref, *, top_k):
    half = oa_ref.shape[1]
    ob = (oc_ref[...] + os_ref[...] + ow_ref[...]).astype(BF16)
    h1 = h_ref[...] + _nn(oa_ref[...], wo_ref[0:half, :]) + _nn(ob, wo_ref[half:2 * half, :])
    h1_ref[...] = h1
    xn = h1 * lax.rsqrt(jnp.mean(h1 * h1, axis=-1, keepdims=True) + RMS_EPS) * gf_ref[...]
    xn_ref[...] = xn
    hi, lo = _split(xn)
    logits = _nn(hi, wrh_ref[...]) + _nn(lo, wrh_ref[...]) + _nn(hi, wrl_ref[...]) + br_ref[...]
    lane = _lane(logits.shape)
    lane_f = lane.astype(F32)
    score = logits
    tops, es = [], []
    for _ in range(top_k):
        m = jnp.max(score, axis=-1, keepdims=True)
        first = jnp.min(jnp.where(score == m, lane_f, 1e9), axis=-1, keepdims=True)
        score = jnp.where(lane_f == first, REMOVED, score)
        tops.append(m)
        es.append(first)
    ws = [jnp.exp(t - tops[0]) for t in tops]
    tot = functools.reduce(lambda a, b2: a + b2, ws)
    e_out = jnp.zeros(logits.shape, jnp.int32)
    w_out = jnp.zeros(logits.shape, F32)
    for k in range(top_k):
        e_out = jnp.where(lane == k, es[k].astype(jnp.int32), e_out)
        w_out = jnp.where(lane == k, ws[k] / tot, w_out)
    e_ref[...] = e_out
    gw_ref[...] = w_out


def _outproj(h, oa, oc, os_, ow, wo, gf, wrh, wrl, br, *, tm, top_k):
    n, d = h.shape
    row = lambda i: (i, 0)
    full = lambda i: (0, 0)
    return pl.pallas_call(
        functools.partial(_outproj_kernel, top_k=top_k), name="outproj",
        grid=(n // tm,),
        in_specs=[pl.BlockSpec((tm, d), row), pl.BlockSpec((tm, oa.shape[1]), row),
                  pl.BlockSpec((tm, oc.shape[1]), row), pl.BlockSpec((tm, oc.shape[1]), row),
                  pl.BlockSpec((tm, oc.shape[1]), row),
                  pl.BlockSpec(wo.shape, full), pl.BlockSpec(gf.shape, full),
                  pl.BlockSpec(wrh.shape, full), pl.BlockSpec(wrl.shape, full), pl.BlockSpec(br.shape, full)],
        out_specs=[pl.BlockSpec((tm, d), row), pl.BlockSpec((tm, d), row),
                   pl.BlockSpec((tm, LANES), row), pl.BlockSpec((tm, LANES), row)],
        out_shape=[jax.ShapeDtypeStruct((n, d), F32), jax.ShapeDtypeStruct((n, d), F32),
                   jax.ShapeDtypeStruct((n, LANES), jnp.int32), jax.ShapeDtypeStruct((n, LANES), F32)],
        compiler_params=_cparams(("parallel",)),
    )(h, oa, oc, os_, ow, wo, gf, wrh, wrl, br)


def _gather_kernel(idx_ref, src_ref, o_ref, sem, *, rows):
    g = pl.program_id(0)

    def start(r, c):
        pltpu.make_async_copy(src_ref.at[pl.ds(idx_ref[g * rows + r], 1)], o_ref.at[pl.ds(r, 1)], sem.at[0]).start()
        return c

    lax.fori_loop(0, rows, start, 0, unroll=8)
    pltpu.make_async_copy(src_ref.at[pl.ds(0, rows)], o_ref, sem.at[0]).wait()


def _gather_rows(idx, src, *, rows):
    n = idx.shape[0]
    d = src.shape[1]
    grid_spec = pltpu.PrefetchScalarGridSpec(
        num_scalar_prefetch=1, grid=(n // rows,),
        in_specs=[pl.BlockSpec(memory_space=pl.ANY)],
        out_specs=pl.BlockSpec((rows, d), lambda g, idx_: (g, 0)),
        scratch_shapes=[pltpu.SemaphoreType.DMA((1,))],
    )
    return pl.pallas_call(
        functools.partial(_gather_kernel, rows=rows), name="gather_rows",
        grid_spec=grid_spec,
        out_shape=jax.ShapeDtypeStruct((n, d), src.dtype),
        compiler_params=_cparams(("arbitrary",)),
    )(idx, src)


def _expert_kernel(be_ref, nu_ref, x_ref, w1_ref, b1_ref, w2_ref, b2_ref, o_ref, w1s, w2s, *, d_ff):
    i = pl.program_id(0)
    prev = be_ref[jnp.maximum(i - 1, 0)]

    @pl.when((i == 0) | (be_ref[i] != prev))
    def _():
        w1s[...] = w1_ref[0].astype(BF16)
        w2s[...] = w2_ref[0].astype(BF16)

    @pl.when(i < nu_ref[0])
    def _():
        hb = _nn(x_ref[...].astype(BF16), w1s[...]) + b1_ref[0]
        g = jnp.minimum(hb[:, 0:d_ff], SWIGLU_LIMIT)
        u = jnp.clip(hb[:, d_ff:2 * d_ff], -SWIGLU_LIMIT, SWIGLU_LIMIT)
        act = (u + 1.0) * g * jax.nn.sigmoid(SWIGLU_ALPHA * g)
        o_ref[...] = _nn(act.astype(BF16), w2s[...]) + b2_ref[0]

    @pl.when(i >= nu_ref[0])
    def _():
        o_ref[...] = jnp.zeros(o_ref.shape, F32)


def _experts(block_e, n_used, x_sorted, w1, b1, w2, b2, *, bm):
    r, d = x_sorted.shape
    d_ff = w2.shape[1]
    grid_spec = pltpu.PrefetchScalarGridSpec(
        num_scalar_prefetch=2, grid=(r // bm,),
        in_specs=[
            pl.BlockSpec((bm, d), lambda i, be, nu: (i, 0)),
            pl.BlockSpec((1,) + w1.shape[1:], lambda i, be, nu: (be[i], 0, 0)),
            pl.BlockSpec((1, 1, b1.shape[2]), lambda i, be, nu: (be[i], 0, 0)),
            pl.BlockSpec((1,) + w2.shape[1:], lambda i, be, nu: (be[i], 0, 0)),
            pl.BlockSpec((1, 1, b2.shape[2]), lambda i, be, nu: (be[i], 0, 0)),
        ],
        out_specs=pl.BlockSpec((bm, d), lambda i, be, nu: (i, 0)),
        scratch_shapes=[pltpu.VMEM(w1.shape[1:], BF16), pltpu.VMEM(w2.shape[1:], BF16)],
    )
    return pl.pallas_call(
        functools.partial(_expert_kernel, d_ff=d_ff), name="experts",
        grid_spec=grid_spec,
        out_shape=jax.ShapeDtypeStruct((r, d), F32),
        compiler_params=_cparams(("arbitrary",)),
    )(block_e, n_used, x_sorted, w1, b1, w2, b2)


def _final_kernel(h1_ref, y_ref, gw_ref, p_ref, wp_ref, gp_ref, wg_ref, gfin_ref, o_ref, *, top_k):
    h2 = h1_ref[...]
    gw = gw_ref[...]
    for k in range(top_k):
        h2 = h2 + y_ref[k] * gw[:, k:k + 1]
    hn = h2 * lax.rsqrt(jnp.mean(h2 * h2, axis=-1, keepdims=True) + RMS_EPS) * gp_ref[...]
    gate = jax.nn.sigmoid(_nn(hn.astype(BF16), wg_ref[...]))
    h3 = h2 + _nn(p_ref[...].astype(BF16), wp_ref[...]) * gate
    o_ref[...] = h3 * lax.rsqrt(jnp.mean(h3 * h3, axis=-1, keepdims=True) + RMS_EPS) * gfin_ref[...]


def _final(h1, y4, gate_w, p, wp, gp, wg, gfin, *, tm, top_k):
    n, d = h1.shape
    row = lambda i: (i, 0)
    full = lambda i: (0, 0)
    return pl.pallas_call(
        functools.partial(_final_kernel, top_k=top_k), name="final",
        grid=(n // tm,),
        in_specs=[pl.BlockSpec((tm, d), row), pl.BlockSpec((top_k, tm, d), lambda i: (0, i, 0)),
                  pl.BlockSpec((tm, LANES), row), pl.BlockSpec((tm, p.shape[1]), row),
                  pl.BlockSpec(wp.shape, full), pl.BlockSpec(gp.shape, full), pl.BlockSpec(wg.shape, full),
                  pl.BlockSpec(gfin.shape, full)],
        out_specs=pl.BlockSpec((tm, d), row),
        out_shape=jax.ShapeDtypeStruct((n, d), F32),
        compiler_params=_cparams(("parallel",)),
    )(h1, y4, gate_w, p, wp, gp, wg, gfin)


def _rope_tables(pos, dim, reps, width):
    inv = ROPE_THETA ** (-jnp.arange(0, dim, 2, dtype=F32) / dim)
    ang = pos.astype(F32)[:, None] * inv[None, :]
    cos, sin = jnp.cos(ang), jnp.sin(ang)
    c = jnp.tile(jnp.concatenate([cos, cos], axis=1), (1, reps))
    s = jnp.tile(jnp.concatenate([-sin, sin], axis=1), (1, reps))
    pad = width - c.shape[1]
    return jnp.pad(c, ((0, 0), (0, pad))), jnp.pad(s, ((0, 0), (0, pad)))


def _tile_rows(n, pref):
    for t in (pref, 256, 128, 64, 32, 16, 8):
        if t <= pref and n % t == 0:
            return t
    return n


def kernel(x_prompt, x_sample, cache_mla, cache_nsa_cmp, cache_nsa_sel, state_nsa_win, page_table, p_prompt, p_sample,
           g_attn, w_in, g_q, w_uq, g_kv, w_uk, w_uv, cmp_pe, w_cmp1, w_cmp2, w_o, g_ffn, w_router, b_router,
           w_e_in, b_e_in, w_e_out, b_e_out, g_ple, w_ple, w_ple_gate, g_final):
    depth = g_attn.shape[0]
    assert depth == 1
    b, t, d = x_prompt.shape
    bs, ts, _ = x_sample.shape
    n_pages = page_table.shape[1]
    page_rows = cache_mla.shape[2]
    past = n_pages * page_rows
    lat = cache_mla.shape[3]
    q_rank, n_heads, qk_dim = w_uq.shape[1:]
    kv_rank, _, nope = w_uk.shape[1:]
    rope_a = qk_dim - nope
    v_dim = w_uv.shape[3]
    cmp_block = cmp_pe.shape[1]
    kvh, hd = cache_nsa_cmp.shape[4], cache_nsa_cmp.shape[5]
    kv_cols = 2 * kvh * hd
    n_q = 8
    grp = n_q // kvh
    window = 512
    n_exp = w_router.shape[2]
    d_ff = w_e_out.shape[2]
    assert (kvh, hd, rope_a, kv_rank, q_rank, kv_cols) == (2, 64, 32, 128, 256, 256)
    np_, ns_ = b * t, bs * ts
    n = np_ + ns_
    mla_scale = float(qk_dim) ** -0.5
    nsa_scale = float(hd) ** -0.5

    wi = w_in[0]
    o_cq, o_ckv, o_kr, o_qb, o_cmp, o_sel, o_win, o_gate = 0, 256, 384, 416, 928, 1184, 1440, 1696
    misc_w = jnp.concatenate([wi[:, o_kr:o_kr + 32], wi[:, o_gate:o_gate + 24], jnp.zeros((d, 72), F32)], axis=1)
    win_r = jnp.concatenate([wi[:, o_cq:o_cq + 256], wi[:, o_ckv:o_ckv + 128], wi[:, o_qb:o_qb + 512],
                             wi[:, o_cmp:o_cmp + 256], wi[:, o_sel:o_sel + 256], wi[:, o_win:o_win + 256],
                             misc_w], axis=1).astype(BF16)
    wnope = jnp.transpose(w_uq[0][:, :, :nope], (1, 0, 2))
    wuk = jnp.transpose(w_uk[0], (1, 0, 2))
    wrope = jnp.transpose(w_uq[0][:, :, nope:], (1, 0, 2))
    wrope_p = jnp.concatenate([wrope[:, :, rope_a // 2:], wrope[:, :, :rope_a // 2]], axis=2)
    pad_r = lambda w: jnp.transpose(jnp.pad(w, ((0, 0), (0, 0), (0, LANES - rope_a))), (1, 0, 2)).reshape(
        q_rank, n_heads * LANES).astype(BF16)
    wr, wrp = pad_r(wrope), pad_r(wrope_p)
    wuv_t = jnp.transpose(w_uv[0], (1, 0, 2))
    zv = jnp.zeros_like(wuv_t)
    odd = (jnp.arange(n_heads) % 2 == 1)[:, None, None]
    wuv_h = jnp.concatenate([jnp.where(odd, zv, wuv_t), jnp.where(odd, wuv_t, zv)], axis=2).astype(BF16)
    wuv_flat = w_uv[0].reshape(kv_rank, n_heads * v_dim).astype(BF16)

    pe = cmp_pe[0]
    pe_full = jnp.broadcast_to(pe[:, :, None, :], (cmp_block, 2, kvh, hd)).reshape(cmp_block, kv_cols)
    w1 = w_cmp1[0].reshape(2, cmp_block, hd, -1)
    hid = w1.shape[3]
    eye = jnp.eye(kvh, dtype=F32)
    w1bd = jnp.einsum('srdh,kK->srkdKh', w1, eye).reshape(2, cmp_block, kvh * hd, kvh * hid).astype(BF16)
    w2bd = jnp.einsum('shd,kK->skhKd', w_cmp2[0], eye).reshape(2, kvh * hid, kvh * hd).astype(BF16)

    wo = w_o[0].astype(BF16)
    wrt = jnp.pad(w_router[0], ((0, 0), (0, LANES - n_exp)))
    wrh = wrt.astype(BF16)
    wrl = (wrt - wrh.astype(F32)).astype(BF16)
    br = jnp.concatenate([b_router[0], jnp.full((LANES - n_exp,), NEG_INF, F32)])[None, :]

    pos = jnp.concatenate([jnp.tile(jnp.arange(t, dtype=jnp.int32), b),
                           jnp.tile(past + jnp.arange(ts, dtype=jnp.int32), bs)])
    cosa, sina = _rope_tables(pos, rope_a, 1, LANES)
    cosb, sinb = _rope_tables(pos, hd, 2, LANES)

    x = jnp.concatenate([x_prompt.reshape(np_, d), x_sample.reshape(ns_, d)], axis=0)
    tm = _tile_rows(n, 256)
    (rows_a, kmla, qmla, rows_c, rows_s, rows_w, kvs, kvw, qraw, qrope, gates) = _proj(
        x, cosa, sina, cosb, sinb, g_attn, win_r, g_q, g_kv, wnope, wuk, wr, wrp,
        tm=tm, mla_scale=mla_scale, nsa_scale=nsa_scale)

    tq_a = _tile_rows(t, 128)
    tk_a = _tile_rows(t, 512)
    oa_p = _mla_prompt(qmla, kmla, wuv_h, b=b, t=t, tq=tq_a, tk=tk_a)

    pages_c = 32
    cpages = np_ // page_rows
    rows_c_pages = rows_c[:np_].reshape(cpages, page_rows, kv_cols)
    blocks_p = _compress(jnp.arange(cpages, dtype=jnp.int32), rows_c_pages, pe_full, w1bd, w2bd,
                         pages=min(pages_c, cpages), transposed=False)
    tq_b = _tile_rows(t, 256)
    oc_p, selmask = _cmp_prompt(qraw, blocks_p, gates, b=b, t=t, tq=tq_b, cmp_block=cmp_block)
    os_p = _sel_prompt(qrope, kvs, selmask, gates, b=b, t=t, tq=tq_b, tk=_tile_rows(t, 512), cmp_block=cmp_block)
    ow_p = _win_prompt(qrope, kvw, gates, b=b, t=t, tq=tq_b, window=window)

    q_s = jnp.transpose(qmla[:, np_:, :].reshape(n_heads, bs, ts, 256), (1, 0, 2, 3)).reshape(bs, n_heads * ts, 256)
    knew = jnp.pad(kmla[np_:].reshape(bs, ts, 256), ((0, 0), (0, 16 - ts), (0, 0)))
    rows_last = lambda c: jnp.transpose(c, (0, 2, 3, 4, 1)).reshape(c.shape[0], kv_cols, c.shape[1])
    cache_mla_t = jnp.transpose(cache_mla[0], (0, 2, 1))
    cache_cmp_t = rows_last(cache_nsa_cmp[0])
    cache_sel_t = rows_last(cache_nsa_sel[0])
    win_t = rows_last(state_nsa_win[0])
    pages_a = min(16, n_pages)
    oa_s = _mla_sample(page_table, q_s, knew, wuv_flat, cache_mla_t, pages=pages_a, ts=ts).reshape(ns_, -1)

    sum_past = _compress(page_table.reshape(-1), cache_cmp_t, pe_full, w1bd, w2bd,
                         pages=min(pages_c, n_pages), transposed=True)
    per_page_c = page_rows // cmp_block
    new_len = -(-ts // SEL_BLOCK) * SEL_BLOCK
    new_pages = -(-new_len // page_rows)
    rc_new = jnp.pad(rows_c[np_:].reshape(bs, ts, kv_cols), ((0, 0), (0, new_pages * page_rows - ts), (0, 0)))
    sum_new = _compress(jnp.arange(bs * new_pages, dtype=jnp.int32), rc_new.reshape(bs * new_pages, page_rows, kv_cols),
                        pe_full, w1bd, w2bd, pages=min(pages_c, bs * new_pages), transposed=False)
    nc_s = past // cmp_block + new_len // cmp_block
    nc_pad = -(-nc_s // LANES) * LANES
    blocks_s = jnp.concatenate([
        sum_past.reshape(bs, n_pages * per_page_c, kv_cols),
        sum_new.reshape(bs, new_pages * per_page_c, kv_cols)[:, :new_len // cmp_block],
        jnp.zeros((bs, nc_pad - nc_s, kv_cols), F32)], axis=1)
    to_s = lambda q: jnp.transpose(q[:, np_:, :].reshape(kvh, grp, bs, ts, LANES), (2, 0, 3, 1, 4)).reshape(
        bs, kvh * ts * grp, LANES)
    grow = jnp.transpose(gates[np_:, 32:32 + 3 * n_q].reshape(bs, ts, kvh, grp, 3), (0, 2, 1, 3, 4)).reshape(
        bs, kvh * ts * grp, 3)
    grow = jnp.pad(grow, ((0, 0), (0, 0), (0, LANES - 3)))
    n_sel = min(N_SEL, nc_s * cmp_block // SEL_BLOCK)
    nb_c = 8 if bs % 8 == 0 else 1
    oc_s, idx = _cmp_sample(to_s(qraw), blocks_s, grow, ts=ts, past=past, cmp_block=cmp_block, n_sel=n_sel, grp=grp,
                            nb=nb_c)
    idx_flat = idx[:, :, :n_sel].reshape(-1)
    new_t = lambda r: jnp.pad(jnp.transpose(r[np_:].reshape(bs, ts, kv_cols), (0, 2, 1)),
                              ((0, 0), (0, 0), (0, page_rows - ts)))
    q_c = jnp.concatenate([qrope[:grp, np_:, :64], qrope[grp:, np_:, 64:]], axis=0)
    q_c = jnp.transpose(q_c.reshape(kvh, grp, bs, ts, 64), (2, 0, 3, 1, 4)).reshape(bs, kvh * ts * grp, 64)
    os_s, ow_s = _selwin_sample(idx_flat, page_table, q_c, win_t, new_t(rows_w), grow, cache_sel_t, new_t(rows_s),
                                ts=ts, grp=grp, n_sel=n_sel, past_blocks=past // SEL_BLOCK,
                                per_page=page_rows // SEL_BLOCK)
    from_s = lambda o: jnp.transpose(o.reshape(bs, kvh, ts, grp, 64), (0, 2, 1, 3, 4)).reshape(ns_, n_q * 64)

    cat = lambda p_, s_: jnp.concatenate([p_, s_.reshape(ns_, -1).astype(p_.dtype)], axis=0)
    h1, xn, top_e, gate_w = _outproj(x, cat(oa_p, oa_s), cat(oc_p, from_s(oc_s)), cat(os_p, from_s(os_s)),
                                     cat(ow_p, from_s(ow_s)), wo, g_ffn, wrh, wrl, br, tm=tm, top_k=TOP_K)

    bm = 256
    flat_e = top_e[:, :TOP_K].reshape(-1)
    n_assign = flat_e.shape[0]
    onehot = (flat_e[:, None] == jnp.arange(n_exp, dtype=jnp.int32)[None, :]).astype(jnp.int32)
    rank = jnp.take_along_axis(jnp.cumsum(onehot, axis=0), flat_e[:, None], axis=1)[:, 0] - 1
    counts = jnp.sum(onehot, axis=0)
    padded = (counts + bm - 1) // bm * bm
    pad_end = jnp.cumsum(padded)
    dest = (pad_end - padded)[flat_e] + rank
    n_blocks = -(-n_assign // bm) + n_exp
    row_tok = jnp.zeros((n_blocks * bm,), jnp.int32).at[dest].set(jnp.arange(n_assign, dtype=jnp.int32) // TOP_K)
    block_start = jnp.arange(n_blocks, dtype=jnp.int32) * bm
    block_e = jnp.minimum(jnp.sum((pad_end[None, :] <= block_start[:, None]).astype(jnp.int32), axis=1), n_exp - 1)
    n_used = (pad_end[-1] // bm).astype(jnp.int32)[None]

    x_sorted = _gather_rows(row_tok, xn, rows=bm)
    y_sorted = _experts(block_e, n_used, x_sorted, w_e_in[0], b_e_in[0][:, None, :],
                        w_e_out[0], b_e_out[0][:, None, :], bm=bm)
    dest_k = jnp.transpose(dest.reshape(n, TOP_K)).reshape(-1).astype(jnp.int32)
    y4 = _gather_rows(dest_k, y_sorted, rows=_tile_rows(n, 256)).reshape(TOP_K, n, d)

    p_all = jnp.concatenate([p_prompt[0].reshape(np_, -1), p_sample[0].reshape(ns_, -1)], axis=0)
    y = _final(h1, y4, gate_w, p_all, w_ple[0].astype(BF16), g_ple, w_ple_gate[0].astype(BF16), g_final[None, :],
               tm=tm, top_k=TOP_K)

    kv_shape = (2, kvh, hd)
    wlen = state_nsa_win.shape[2]
    rows_w_p = rows_w[:np_].reshape(b, t, kv_cols)
    win_s = jnp.concatenate([state_nsa_win[0].reshape(bs, wlen, kv_cols), rows_w[np_:].reshape(bs, ts, kv_cols)], axis=1)[:, ts:]
    return (
        y[:np_].reshape(b, t, d), y[np_:].reshape(bs, ts, d),
        rows_a[:np_].reshape(1, b, t, lat), rows_a[np_:].reshape(1, bs, ts, lat),
        rows_c[:np_].reshape((1, b, t) + kv_shape), rows_c[np_:].reshape((1, bs, ts) + kv_shape),
        rows_s[:np_].reshape((1, b, t) + kv_shape), rows_s[np_:].reshape((1, bs, ts) + kv_shape),
        rows_w_p[:, t - min(window, t):].reshape((1, b, min(window, t)) + kv_shape),
        win_s.reshape((1, bs, wlen) + kv_shape),
    )
```

```python
import functools

import jax
import jax.numpy as jnp
from jax import lax
from jax.experimental import pallas as pl
from jax.experimental.pallas import tpu as pltpu

RMS_EPS = 1e-6
ROPE_THETA = 10000.0
NEG_INF = -1e30
REMOVED = -3e38
FORCE_SCORE = 1e4
SEL_BLOCK = 64
N_SEL = 16
TOP_K = 4
SWIGLU_LIMIT = 7.0
SWIGLU_ALPHA = 1.702

LANES = 128
VMEM_LIMIT = 56 * 1024 * 1024

F32 = jnp.float32
BF16 = jnp.bfloat16


def _cparams(sem):
    return pltpu.CompilerParams(dimension_semantics=sem, vmem_limit_bytes=VMEM_LIMIT)


def _nt(a, b):
    return lax.dot_general(a, b, (((1,), (1,)), ((), ())), preferred_element_type=F32)


def _nn(a, b):
    return jnp.dot(a, b, preferred_element_type=F32)


def _split(x):
    hi = x.astype(BF16)
    lo = (x - hi.astype(F32)).astype(BF16)
    return hi, lo


def _lane(shape, dim=None):
    return lax.broadcasted_iota(jnp.int32, shape, len(shape) - 1 if dim is None else dim)


def _partner(x, half):
    w = x.shape[-1]
    lane = _lane(x.shape)
    first = (lane % (2 * half)) < half
    return jnp.where(first, pltpu.roll(x, w - half, x.ndim - 1), pltpu.roll(x, half, x.ndim - 1))


def _swap64(x):
    return pltpu.roll(x, 64, x.ndim - 1)


def _proj_kernel(x_ref, ga_ref, win_ref, gq_ref, gkv_ref, wnope_ref, wuk_ref, wr_ref, wrp_ref,
                 cosa_ref, sina_ref, cosb_ref, sinb_ref,
                 rowsa_ref, kmla_ref, qmla_ref, rowsc_ref, rowss_ref, rowsw_ref,
                 kvs_ref, kvw_ref, qraw_ref, qrope_ref, gates_ref,
                 wabs_ref, *, mla_scale, nsa_scale):
    n_heads = qmla_ref.shape[0]

    @pl.when(pl.program_id(0) == 0)
    def _():
        for h in range(n_heads):
            a_hi, a_lo = _split(wnope_ref[h])
            b_hi, b_lo = _split(wuk_ref[h])
            w = _nt(a_hi, b_hi) + _nt(a_lo, b_hi) + _nt(a_hi, b_lo)
            wabs_ref[:, h * LANES:(h + 1) * LANES] = w.astype(BF16)

    x = x_ref[...]
    xn = x * lax.rsqrt(jnp.mean(x * x, axis=-1, keepdims=True) + RMS_EPS) * ga_ref[...]
    proj = _nn(xn.astype(BF16), win_ref[...])
    cq = proj[:, 0:256]
    ckv = proj[:, 256:384]
    qb = proj[:, 384:896]
    cmp_raw = proj[:, 896:1152]
    sel_raw = proj[:, 1152:1408]
    win_raw = proj[:, 1408:1664]
    misc = proj[:, 1664:1792]

    cosa, sina = cosa_ref[...], sina_ref[...]
    cosb, sinb = cosb_ref[...], sinb_ref[...]

    cqn = (cq * lax.rsqrt(jnp.mean(cq * cq, axis=-1, keepdims=True) + RMS_EPS) * gq_ref[...]).astype(BF16)
    ckvn = ckv * lax.rsqrt(jnp.mean(ckv * ckv, axis=-1, keepdims=True) + RMS_EPS) * gkv_ref[...]
    krope = misc * cosa + _partner(misc, 16) * sina
    rowsa_ref[:, 0:128] = ckvn
    rowsa_ref[:, 128:160] = krope[:, 0:32]
    kmla_ref[...] = jnp.concatenate([ckvn, krope], axis=1).astype(BF16)
    qlat = _nn(cqn, wabs_ref[...])
    qr = _nn(cqn, wr_ref[...])
    qrp = _nn(cqn, wrp_ref[...])
    for h in range(n_heads):
        sl = slice(h * LANES, (h + 1) * LANES)
        rope_h = qr[:, sl] * cosa + qrp[:, sl] * sina
        qmla_ref[h] = (jnp.concatenate([qlat[:, sl], rope_h], axis=1) * mla_scale).astype(BF16)

    rowsc_ref[...] = cmp_raw
    sel_k = sel_raw[:, 0:128] * cosb + _partner(sel_raw[:, 0:128], 32) * sinb
    win_k = win_raw[:, 0:128] * cosb + _partner(win_raw[:, 0:128], 32) * sinb
    rows_s = jnp.concatenate([sel_k, sel_raw[:, 128:256]], axis=1)
    rows_w = jnp.concatenate([win_k, win_raw[:, 128:256]], axis=1)
    rowss_ref[...] = rows_s
    rowsw_ref[...] = rows_w
    kvs_ref[...] = rows_s.astype(BF16)
    kvw_ref[...] = rows_w.astype(BF16)
    cos4 = jnp.concatenate([cosb] * 4, axis=1)
    sin4 = jnp.concatenate([sinb] * 4, axis=1)
    qb_rope = qb * cos4 + _partner(qb, 32) * sin4
    lane = _lane((x.shape[0], LANES))
    n_q = qraw_ref.shape[0]
    for hh in range(n_q):
        kvh = hh // (n_q // 2)
        blk = slice((hh // 2) * LANES, (hh // 2 + 1) * LANES)
        keep = (lane >= 64) if kvh == 1 else (lane < 64)
        for src, dst in ((qb, qraw_ref), (qb_rope, qrope_ref)):
            v = src[:, blk]
            if (hh % 2) != kvh:
                v = _swap64(v)
            dst[hh] = jnp.where(keep, v * nsa_scale, 0.0).astype(BF16)
    gates_ref[...] = jax.nn.sigmoid(misc)


def _proj(x, cosa, sina, cosb, sinb, ga, win_r, gq, gkv, wnope, wuk, wr, wrp, *, tm, mla_scale, nsa_scale):
    n, d = x.shape
    n_heads = wnope.shape[0]
    grid = (n // tm,)
    row = lambda i: (i, 0)
    full2 = lambda i: (0, 0)
    full3 = lambda i: (0, 0, 0)
    head_row = lambda i: (0, i, 0)
    in_specs = [
        pl.BlockSpec((tm, d), row),
        pl.BlockSpec(ga.shape, full2),
        pl.BlockSpec(win_r.shape, full2),
        pl.BlockSpec(gq.shape, full2),
        pl.BlockSpec(gkv.shape, full2),
        pl.BlockSpec(wnope.shape, full3),
        pl.BlockSpec(wuk.shape, full3),
        pl.BlockSpec(wr.shape, full2),
        pl.BlockSpec(wrp.shape, full2),
        pl.BlockSpec((tm, LANES), row), pl.BlockSpec((tm, LANES), row),
        pl.BlockSpec((tm, LANES), row), pl.BlockSpec((tm, LANES), row),
    ]
    out_shape = [
        jax.ShapeDtypeStruct((n, 160), F32),
        jax.ShapeDtypeStruct((n, 256), BF16),
        jax.ShapeDtypeStruct((n_heads, n, 256), BF16),
        jax.ShapeDtypeStruct((n, 256), F32),
        jax.ShapeDtypeStruct((n, 256), F32),
        jax.ShapeDtypeStruct((n, 256), F32),
        jax.ShapeDtypeStruct((n, 256), BF16),
        jax.ShapeDtypeStruct((n, 256), BF16),
        jax.ShapeDtypeStruct((8, n, LANES), BF16),
        jax.ShapeDtypeStruct((8, n, LANES), BF16),
        jax.ShapeDtypeStruct((n, LANES), F32),
    ]
    out_specs = [
        pl.BlockSpec((tm, 160), row),
        pl.BlockSpec((tm, 256), row),
        pl.BlockSpec((n_heads, tm, 256), head_row),
        pl.BlockSpec((tm, 256), row),
        pl.BlockSpec((tm, 256), row),
        pl.BlockSpec((tm, 256), row),
        pl.BlockSpec((tm, 256), row),
        pl.BlockSpec((tm, 256), row),
        pl.BlockSpec((8, tm, LANES), head_row),
        pl.BlockSpec((8, tm, LANES), head_row),
        pl.BlockSpec((tm, LANES), row),
    ]
    return pl.pallas_call(
        functools.partial(_proj_kernel, mla_scale=mla_scale, nsa_scale=nsa_scale), name="proj",
        grid=grid, in_specs=in_specs, out_specs=out_specs, out_shape=out_shape,
        scratch_shapes=[pltpu.VMEM((wnope.shape[1], n_heads * LANES), BF16)],
        compiler_params=_cparams(("arbitrary",)),
    )(x, ga, win_r, gq, gkv, wnope, wuk, wr, wrp, cosa, sina, cosb, sinb)


def _mla_prompt_kernel(q_ref, k_ref, wuv_ref, o_ref, m_scr, l_scr, acc_scr, *, tq, tk):
    i, j = pl.program_id(1), pl.program_id(2)
    nh = q_ref.shape[0]
    jmax = (i * tq + tq - 1) // tk

    @pl.when(j == 0)
    def _():
        m_scr[...] = jnp.full(m_scr.shape, NEG_INF, F32)
        l_scr[...] = jnp.zeros(l_scr.shape, F32)
        acc_scr[...] = jnp.zeros(acc_scr.shape, F32)

    def step(masked):
        q = q_ref[...].reshape(nh * tq, q_ref.shape[2])
        k = k_ref[...]
        s = _nt(q, k)
        if masked:
            qpos = i * tq + lax.broadcasted_iota(jnp.int32, (1, tq, tk), 1)
            kpos = j * tk + lax.broadcasted_iota(jnp.int32, (1, tq, tk), 2)
            s = jnp.where(kpos <= qpos, s.reshape(nh, tq, tk), NEG_INF).reshape(nh * tq, tk)
        m_old = m_scr[...]
        m_new = jnp.maximum(m_old, jnp.max(s, axis=-1, keepdims=True))
        alpha = jnp.exp(m_old - m_new)
        p = jnp.exp(s - m_new)
        l_scr[...] = l_scr[...] * alpha + jnp.sum(p, axis=-1, keepdims=True)
        acc_scr[...] = acc_scr[...] * alpha + _nn(p.astype(BF16), k[:, 0:LANES])
        m_scr[...] = m_new

    full = (j + 1) * tk - 1 <= i * tq
    pl.when(full)(lambda: step(False))
    pl.when(jnp.logical_and(j <= jmax, jnp.logical_not(full)))(lambda: step(True))

    @pl.when(j == pl.num_programs(2) - 1)
    def _():
        o = (acc_scr[...] / l_scr[...]).astype(BF16).reshape(nh, tq, LANES)
        for m in range(nh // 2):
            pair = _nn(o[2 * m], wuv_ref[2 * m]) + _nn(o[2 * m + 1], wuv_ref[2 * m + 1])
            o_ref[:, m * LANES:(m + 1) * LANES] = pair.astype(o_ref.dtype)


def _mla_prompt(qmla, kmla, wuv, *, b, t, tq, tk):
    nh = qmla.shape[0]
    nq, nk = t // tq, t // tk
    out_w = nh * wuv.shape[2] // 2

    def kmap(bb, i, j):
        return (bb * nk + jnp.minimum(j, (i * tq + tq - 1) // tk), 0)

    return pl.pallas_call(
        functools.partial(_mla_prompt_kernel, tq=tq, tk=tk), name="mla_prompt",
        grid=(b, nq, nk),
        in_specs=[
            pl.BlockSpec((nh, tq, qmla.shape[2]), lambda bb, i, j: (0, bb * nq + i, 0)),
            pl.BlockSpec((tk, kmla.shape[1]), kmap),
            pl.BlockSpec(wuv.shape, lambda bb, i, j: (0, 0, 0)),
        ],
        out_specs=pl.BlockSpec((tq, out_w), lambda bb, i, j: (bb * nq + i, 0)),
        out_shape=jax.ShapeDtypeStruct((b * t, out_w), BF16),
        scratch_shapes=[pltpu.VMEM((nh * tq, 1), F32), pltpu.VMEM((nh * tq, 1), F32),
                        pltpu.VMEM((nh * tq, LANES), F32)],
        compiler_params=_cparams(("arbitrary", "arbitrary", "arbitrary")),
    )(qmla, kmla, wuv)


def _page_pipeline(g, copies):
    n = pl.num_programs(0)
    slot = g % 2

    @pl.when(g == 0)
    def _():
        for c in copies(0, 0):
            c.start()

    @pl.when(g + 1 < n)
    def _():
        for c in copies(g + 1, 1 - slot):
            c.start()

    for c in copies(g, slot):
        c.wait()
    return slot


def _compress_contract(load, w1_ref, w2_ref, o_ref, nblk, cmp_block):
    for s in range(2):
        acc = jnp.zeros((nblk, w1_ref.shape[3]), F32)
        for r in range(cmp_block):
            acc = acc + _nn(load(s, r), w1_ref[s, r])
        hid = jax.nn.gelu(acc)
        o_ref[:, s * LANES:(s + 1) * LANES] = _nn(hid.astype(BF16), w2_ref[s])


def _compress_kernel(tab_ref, src_ref, pe_ref, w1_ref, w2_ref, o_ref, buf, sem, *, pages, page_rows, cmp_block):
    g = pl.program_id(0)

    def copies(step, sl):
        return [pltpu.make_async_copy(src_ref.at[tab_ref[step * pages + p], :, pl.ds(s * LANES, LANES)],
                                      buf.at[sl, s, pl.ds(p * page_rows, page_rows), :], sem.at[sl])
                for p in range(pages) for s in range(2)]

    slot = _page_pipeline(g, copies)
    nblk = pages * page_rows // cmp_block

    def load(s, r):
        xr = buf[slot, s, pl.ds(r, nblk, stride=cmp_block), :] + pe_ref[r:r + 1, s * LANES:(s + 1) * LANES]
        return xr.astype(BF16)

    _compress_contract(load, w1_ref, w2_ref, o_ref, nblk, cmp_block)


def _compress_t_kernel(tab_ref, src_ref, pet_ref, perm_ref, w1_ref, w2_ref, o_ref, buf, sem, xr, *,
                       pages, page_rows, cmp_block):
    g = pl.program_id(0)

    def copies(step, sl):
        return [pltpu.make_async_copy(src_ref.at[tab_ref[step * pages + p]], buf.at[sl, p], sem.at[sl])
                for p in range(pages)]

    slot = _page_pipeline(g, copies)
    per_pair = 2 * page_rows // cmp_block
    perm = perm_ref[...]
    pet = pet_ref[...]
    for q in range(pages // 2):
        xt = jnp.concatenate([buf[slot, 2 * q], buf[slot, 2 * q + 1]], axis=1)
        xb = (xt + pet).astype(BF16)
        rows = _nt(perm, xb)
        xr[:, q * per_pair:(q + 1) * per_pair, :] = rows.reshape(cmp_block, per_pair, rows.shape[1])
    nblk = pages * page_rows // cmp_block
    load = lambda s, r: xr[r, :, s * LANES:(s + 1) * LANES].astype(BF16)
    _compress_contract(load, w1_ref, w2_ref, o_ref, nblk, cmp_block)


def _compress(table, src_pages, pe_full, w1bd, w2bd, *, pages, transposed):
    n_pages = table.shape[0]
    cmp_block, width = pe_full.shape
    page_rows = src_pages.shape[2] if transposed else src_pages.shape[1]
    per_page = page_rows // cmp_block
    const = []
    if transposed:
        assert page_rows == LANES and pages % 2 == 0 and 2 * per_page == 8
        pet = jnp.tile(jnp.transpose(pe_full), (1, 2 * per_page))
        src_lane = jnp.arange(2 * page_rows)
        dst_row = (src_lane % cmp_block) * (2 * per_page) + src_lane // cmp_block
        perm = (jnp.arange(2 * page_rows)[:, None] == dst_row[None, :]).astype(BF16)
        const = [pet, perm]
        body = _compress_t_kernel
        scratch = [pltpu.VMEM((2, pages, width, page_rows), F32), pltpu.SemaphoreType.DMA((2,)),
                   pltpu.VMEM((cmp_block, pages * per_page, width), F32)]
    else:
        const = [pe_full]
        body = _compress_kernel
        scratch = [pltpu.VMEM((2, 2, pages * page_rows, LANES), F32), pltpu.SemaphoreType.DMA((2,))]
    grid_spec = pltpu.PrefetchScalarGridSpec(
        num_scalar_prefetch=1, grid=(n_pages // pages,),
        in_specs=[pl.BlockSpec(memory_space=pl.ANY)]
                 + [pl.BlockSpec(c.shape, lambda g, tab: (0, 0)) for c in const]
                 + [pl.BlockSpec(w1bd.shape, lambda g, tab: (0, 0, 0, 0)),
                    pl.BlockSpec(w2bd.shape, lambda g, tab: (0, 0, 0))],
        out_specs=pl.BlockSpec((pages * per_page, width), lambda g, tab: (g, 0)),
        scratch_shapes=scratch,
    )
    return pl.pallas_call(
        functools.partial(body, pages=pages, page_rows=page_rows, cmp_block=cmp_block),
        name="compress_t" if transposed else "compress",
        grid_spec=grid_spec,
        out_shape=jax.ShapeDtypeStruct((n_pages * per_page, width), F32),
        compiler_params=_cparams(("arbitrary",)),
    )(table, src_pages, *const, w1bd, w2bd)


def _cmp_softmax(q, kc, vc, valid):
    s = jnp.where(valid, _nt(q, kc), NEG_INF)
    m = jnp.max(s, axis=-1, keepdims=True)
    e = jnp.where(valid, jnp.exp(s - m), 0.0)
    l = jnp.sum(e, axis=-1, keepdims=True)
    p = e / jnp.where(l == 0.0, 1.0, l)
    return p, _nn(p.astype(BF16), vc)


def _select(imp, pos, n_sel, cmp_per_sel, cmp_block):
    nc = imp.shape[-1]
    lane = _lane(imp.shape)
    pair = imp
    for d in range(1, cmp_per_sel):
        pair = pair + pltpu.roll(imp, nc - d, imp.ndim - 1)
    j = lane // cmp_per_sel
    qb = pos // (cmp_per_sel * cmp_block)
    forced = (j == 0) | (j == qb) | (j == qb - 1)
    causal = j * (cmp_per_sel * cmp_block) <= pos
    score = jnp.where(causal, jnp.where(forced, FORCE_SCORE, pair), NEG_INF)
    score = jnp.where(lane % cmp_per_sel == 0, score, REMOVED)
    lane_f = lane.astype(F32)
    picks = []
    for _ in range(n_sel):
        m = jnp.max(score, axis=-1, keepdims=True)
        first = jnp.min(jnp.where(score == m, lane_f, 1e9), axis=-1, keepdims=True)
        hit = lane_f == first
        score = jnp.where(hit, REMOVED, score)
        picks.append(first)
    return picks


def _pair_heads(o_even, o_odd, kvh):
    lane = _lane(o_even.shape)
    a = o_even if kvh == 0 else _swap64(o_even)
    b = _swap64(o_odd) if kvh == 0 else o_odd
    return jnp.where(lane < 64, a, b)


def _gate_col(gates, hh, c):
    col = 32 + hh * 3 + c
    return gates[:, col:col + 1]


def _cmp_prompt_kernel(q_ref, bc_ref, g_ref, o_ref, sel_ref, *, tq, cmp_block, n_sel):
    i = pl.program_id(1)
    nc = bc_ref.shape[0]
    gates = g_ref[...]
    bc = bc_ref[...]
    kc = bc[:, 0:LANES].astype(BF16)
    vc = bc[:, LANES:2 * LANES].astype(BF16)
    pos = i * tq + lax.broadcasted_iota(jnp.int32, (tq, 1), 0)
    lane = _lane((tq, nc))
    valid = lane * cmp_block + (cmp_block - 1) <= pos
    cmp_per_sel = SEL_BLOCK // cmp_block
    n_q = q_ref.shape[0]
    grp = n_q // 2
    for kvh in range(2):
        imp = jnp.zeros((tq, nc), F32)
        outs = []
        for g in range(grp):
            hh = kvh * grp + g
            p, o = _cmp_softmax(q_ref[hh], kc, vc, valid)
            imp = imp + p
            outs.append(o * _gate_col(gates, hh, 0))
        for g2 in range(grp // 2):
            blk = kvh * (grp // 2) + g2
            o_ref[:, blk * LANES:(blk + 1) * LANES] = _pair_heads(outs[2 * g2], outs[2 * g2 + 1], kvh)
        picks = _select(imp, pos, n_sel, cmp_per_sel, cmp_block)
        lane_f = lane.astype(F32)
        selm = jnp.zeros((tq, nc), F32)
        for first in picks:
            selm = jnp.where(lane_f == first, 1.0, selm)
        sel_ref[kvh] = selm.astype(BF16)


def _cmp_prompt(qraw, blocks_c, gates, *, b, t, tq, cmp_block):
    n_q = qraw.shape[0]
    nq = t // tq
    nc = t // cmp_block
    n_sel = min(N_SEL, t // SEL_BLOCK)
    return pl.pallas_call(
        functools.partial(_cmp_prompt_kernel, tq=tq, cmp_block=cmp_block, n_sel=n_sel), name="cmp_prompt",
        grid=(b, nq),
        in_specs=[
            pl.BlockSpec((n_q, tq, LANES), lambda bb, i: (0, bb * nq + i, 0)),
            pl.BlockSpec((nc, blocks_c.shape[1]), lambda bb, i: (bb, 0)),
            pl.BlockSpec((tq, LANES), lambda bb, i: (bb * nq + i, 0)),
        ],
        out_specs=[
            pl.BlockSpec((tq, n_q * 64), lambda bb, i: (bb * nq + i, 0)),
            pl.BlockSpec((2, tq, nc), lambda bb, i: (0, bb * nq + i, 0)),
        ],
        out_shape=[jax.ShapeDtypeStruct((b * t, n_q * 64), F32),
                   jax.ShapeDtypeStruct((2, b * t, nc), BF16)],
        compiler_params=_cparams(("arbitrary", "arbitrary")),
    )(qraw, blocks_c, gates)


def _sel_prompt_kernel(q_ref, k_ref, v_ref, sel_ref, g_ref, o_ref, m_scr, l_scr, acc_scr, *, tq, tk, cmp_block):
    kvh, i, j = pl.program_id(1), pl.program_id(2), pl.program_id(3)
    grp = q_ref.shape[0]
    nc = sel_ref.shape[2]
    jmax = (i * tq + tq - 1) // tk
    cmp_per_sel = SEL_BLOCK // cmp_block

    @pl.when(j == 0)
    def _():
        m_scr[...] = jnp.full(m_scr.shape, NEG_INF, F32)
        l_scr[...] = jnp.zeros(l_scr.shape, F32)
        acc_scr[...] = jnp.zeros(acc_scr.shape, F32)

    @pl.when(j <= jmax)
    def _():
        q = q_ref[...].reshape(grp * tq, LANES)
        s = _nt(q, k_ref[...]).reshape(grp, tq, tk)
        row_l = lax.broadcasted_iota(jnp.int32, (nc, tk), 0)
        key = j * tk + lax.broadcasted_iota(jnp.int32, (nc, tk), 1)
        expand = (row_l == (key // SEL_BLOCK) * cmp_per_sel).astype(BF16)
        msel = _nn(sel_ref[0], expand)
        qpos = i * tq + lax.broadcasted_iota(jnp.int32, (tq, tk), 0)
        kpos = j * tk + lax.broadcasted_iota(jnp.int32, (tq, tk), 1)
        valid = ((msel > 0.5) & (kpos <= qpos))[None]
        s = jnp.where(valid, s, NEG_INF)
        m_old = m_scr[...].reshape(grp, tq, 1)
        m_new = jnp.maximum(m_old, jnp.max(s, axis=-1, keepdims=True))
        alpha = jnp.exp(m_old - m_new)
        p = jnp.where(valid, jnp.exp(s - m_new), 0.0)
        l_scr[...] = (l_scr[...].reshape(grp, tq, 1) * alpha + jnp.sum(p, axis=-1, keepdims=True)).reshape(grp * tq, 1)
        pv = _nn(p.reshape(grp * tq, tk).astype(BF16), v_ref[...])
        acc_scr[...] = acc_scr[...] * alpha.reshape(grp * tq, 1) + pv
        m_scr[...] = m_new.reshape(grp * tq, 1)

    @pl.when(j == pl.num_programs(3) - 1)
    def _():
        l = l_scr[...]
        o = (acc_scr[...] / jnp.where(l == 0.0, 1.0, l)).reshape(grp, tq, LANES)
        gates = g_ref[...]
        for g2 in range(grp // 2):
            for kk in range(2):
                @pl.when(kvh == kk)
                def _():
                    a = o[2 * g2] * _gate_col(gates, kk * grp + 2 * g2, 1)
                    b2 = o[2 * g2 + 1] * _gate_col(gates, kk * grp + 2 * g2 + 1, 1)
                    o_ref[:, g2 * LANES:(g2 + 1) * LANES] = _pair_heads(a, b2, kk)


def _sel_prompt(qrope, kvs, selmask, gates, *, b, t, tq, tk, cmp_block):
    n_q = qrope.shape[0]
    grp = n_q // 2
    nq, nk = t // tq, t // tk
    nc = selmask.shape[2]

    def kv_map(lane_blk):
        def f(bb, kvh, i, j):
            return (bb * nk + jnp.minimum(j, (i * tq + tq - 1) // tk), lane_blk)
        return f

    return pl.pallas_call(
        functools.partial(_sel_prompt_kernel, tq=tq, tk=tk, cmp_block=cmp_block), name="sel_prompt",
        grid=(b, 2, nq, nk),
        in_specs=[
            pl.BlockSpec((grp, tq, LANES), lambda bb, kvh, i, j: (kvh, bb * nq + i, 0)),
            pl.BlockSpec((tk, LANES), kv_map(0)),
            pl.BlockSpec((tk, LANES), kv_map(1)),
            pl.BlockSpec((1, tq, nc), lambda bb, kvh, i, j: (kvh, bb * nq + i, 0)),
            pl.BlockSpec((tq, LANES), lambda bb, kvh, i, j: (bb * nq + i, 0)),
        ],
        out_specs=pl.BlockSpec((tq, grp * 64), lambda bb, kvh, i, j: (bb * nq + i, kvh)),
        out_shape=jax.ShapeDtypeStruct((b * t, n_q * 64), F32),
        scratch_shapes=[pltpu.VMEM((grp * tq, 1), F32), pltpu.VMEM((grp * tq, 1), F32),
                        pltpu.VMEM((grp * tq, LANES), F32)],
        compiler_params=_cparams(("arbitrary",) * 4),
    )(qrope, kvs, kvs, selmask, gates)


def _win_prompt_kernel(q_ref, k0, k1, k2, v0, v1, v2, g_ref, o_ref, *, tq, window):
    kvh, i = pl.program_id(1), pl.program_id(2)
    grp = q_ref.shape[0]
    q = q_ref[...].reshape(grp * tq, LANES)
    qpos = i * tq + lax.broadcasted_iota(jnp.int32, (tq, tq), 0)
    col = lax.broadcasted_iota(jnp.int32, (tq, tq), 1)
    ss, vs = [], []
    for d, (kr, vr) in enumerate(((k0, v0), (k1, v1), (k2, v2))):
        blk = i - 2 + d
        kpos = blk * tq + col
        valid = ((kpos >= 0) & (kpos <= qpos) & (kpos > qpos - window))[None]
        s = _nt(q, kr[...]).reshape(grp, tq, tq)
        ss.append(jnp.where(valid, s, NEG_INF))
        vs.append(valid)
    s = jnp.concatenate(ss, axis=2)
    valid = jnp.concatenate(vs, axis=2)
    m = jnp.max(s, axis=-1, keepdims=True)
    e = jnp.where(valid, jnp.exp(s - m), 0.0)
    l = jnp.sum(e, axis=-1, keepdims=True)
    p = (e / jnp.where(l == 0.0, 1.0, l)).reshape(grp * tq, 3 * tq).astype(BF16)
    o = _nn(p[:, 0:tq], v0[...]) + _nn(p[:, tq:2 * tq], v1[...]) + _nn(p[:, 2 * tq:3 * tq], v2[...])
    o = o.reshape(grp, tq, LANES)
    gates = g_ref[...]
    for g2 in range(grp // 2):
        for kk in range(2):
            @pl.when(kvh == kk)
            def _():
                a = o[2 * g2] * _gate_col(gates, kk * grp + 2 * g2, 2)
                b2 = o[2 * g2 + 1] * _gate_col(gates, kk * grp + 2 * g2 + 1, 2)
                o_ref[:, g2 * LANES:(g2 + 1) * LANES] = _pair_heads(a, b2, kk)


def _win_prompt(qrope, kvw, gates, *, b, t, tq, window):
    assert window == 2 * tq
    n_q = qrope.shape[0]
    grp = n_q // 2
    nq = t // tq

    def kv_map(d, lane_blk):
        def f(bb, kvh, i):
            return (bb * nq + jnp.maximum(i - 2 + d, 0), lane_blk)
        return f

    kv_specs = [pl.BlockSpec((tq, LANES), kv_map(d, lb)) for lb in (0, 1) for d in range(3)]
    return pl.pallas_call(
        functools.partial(_win_prompt_kernel, tq=tq, window=window), name="win_prompt",
        grid=(b, 2, nq),
        in_specs=[pl.BlockSpec((grp, tq, LANES), lambda bb, kvh, i: (kvh, bb * nq + i, 0))] + kv_specs
                 + [pl.BlockSpec((tq, LANES), lambda bb, kvh, i: (bb * nq + i, 0))],
        out_specs=pl.BlockSpec((tq, grp * 64), lambda bb, kvh, i: (bb * nq + i, kvh)),
        out_shape=jax.ShapeDtypeStruct((b * t, n_q * 64), F32),
        compiler_params=_cparams(("arbitrary",) * 3),
    )(qrope, kvw, kvw, kvw, kvw, kvw, kvw, gates)


def _mla_sample_kernel(pt_ref, q_ref, knew_ref, wuv_ref, cache_ref, o_ref, buf, sem, *,
                       pages, n_chunks, page_rows, ts, lat):
    b = pl.program_id(0)
    nb = pl.num_programs(0)
    rows = q_ref.shape[1]

    def copies(step, sl):
        return [pltpu.make_async_copy(cache_ref.at[pt_ref[step * pages + p]], buf.at[sl, p], sem.at[sl])
                for p in range(pages)]

    @pl.when(b == 0)
    def _():
        for c in copies(0, 0):
            c.start()

    q = q_ref[0]
    q_lat, q_rope = q[:, 0:LANES], q[:, LANES:lat]

    def chunk(c, carry):
        m_old, l_old, acc = carry
        step = b * n_chunks + c
        slot = step % 2

        @pl.when(step + 1 < nb * n_chunks)
        def _():
            for cp in copies(step + 1, 1 - slot):
                cp.start()

        for cp in copies(step, slot):
            cp.wait()
        kb = [buf[slot, pg].astype(BF16) for pg in range(pages)]
        s = jnp.concatenate([_nn(q_lat, k[0:LANES, :]) + _nn(q_rope, k[LANES:lat, :]) for k in kb], axis=1)
        m_new = jnp.maximum(m_old, jnp.max(s, axis=-1, keepdims=True))
        alpha = jnp.exp(m_old - m_new)
        p = jnp.exp(s - m_new)
        l_new = l_old * alpha + jnp.sum(p, axis=-1, keepdims=True)
        pb = p.astype(BF16)
        acc = acc * alpha
        for pg, k in enumerate(kb):
            acc = acc + _nt(pb[:, pg * page_rows:(pg + 1) * page_rows], k[0:LANES, :])
        return m_new, l_new, acc

    init = (jnp.full((rows, 1), NEG_INF, F32), jnp.zeros((rows, 1), F32), jnp.zeros((rows, LANES), F32))
    m_old, l_old, acc = lax.fori_loop(0, n_chunks, chunk, init)

    kn = knew_ref[0]
    s = _nt(q, kn)
    tok = lax.broadcasted_iota(jnp.int32, s.shape, 0) % ts
    col = lax.broadcasted_iota(jnp.int32, s.shape, 1)
    valid = (col <= tok) & (col < ts)
    s = jnp.where(valid, s, NEG_INF)
    m_new = jnp.maximum(m_old, jnp.max(s, axis=-1, keepdims=True))
    alpha = jnp.exp(m_old - m_new)
    p = jnp.where(valid, jnp.exp(s - m_new), 0.0)
    l_new = l_old * alpha + jnp.sum(p, axis=-1, keepdims=True)
    acc = acc * alpha + _nn(p.astype(BF16), kn[:, 0:LANES])
    o = (acc / l_new).astype(BF16)
    full = _nn(o, wuv_ref[...])
    nh = rows // ts
    vw = wuv_ref.shape[1] // nh
    lane_head = _lane((ts, full.shape[1])) // vw
    out = jnp.zeros((ts, full.shape[1]), F32)
    for h in range(nh):
        out = jnp.where(lane_head == h, full[h * ts:(h + 1) * ts, :], out)
    o_ref[0] = out


def _mla_sample(page_table, q_s, knew, wuv_flat, cache, *, pages, ts):
    bs, n_pages = page_table.shape
    lat, page_rows = cache.shape[1], cache.shape[2]
    n_chunks = n_pages // pages
    rows = q_s.shape[1]
    width = q_s.shape[2]
    grid_spec = pltpu.PrefetchScalarGridSpec(
        num_scalar_prefetch=1, grid=(bs,),
        in_specs=[
            pl.BlockSpec((1, rows, width), lambda b, pt: (b, 0, 0)),
            pl.BlockSpec((1,) + knew.shape[1:], lambda b, pt: (b, 0, 0)),
            pl.BlockSpec(wuv_flat.shape, lambda b, pt: (0, 0)),
            pl.BlockSpec(memory_space=pl.ANY),
        ],
        out_specs=pl.BlockSpec((1, ts, wuv_flat.shape[1]), lambda b, pt: (b, 0, 0)),
        scratch_shapes=[pltpu.VMEM((2, pages, lat, page_rows), F32), pltpu.SemaphoreType.DMA((2,))],
    )
    return pl.pallas_call(
        functools.partial(_mla_sample_kernel, pages=pages, n_chunks=n_chunks, page_rows=page_rows, ts=ts, lat=lat),
        name="mla_sample",
        grid_spec=grid_spec,
        out_shape=jax.ShapeDtypeStruct((bs, ts, wuv_flat.shape[1]), F32),
        compiler_params=_cparams(("arbitrary",)),
    )(page_table.reshape(-1), q_s, knew, wuv_flat, cache)


def _cmp_sample_kernel(q_ref, bc_ref, g_ref, o_ref, idx_ref, *, ts, past, cmp_block, n_sel, grp):
    nb = q_ref.shape[0]
    nc = bc_ref.shape[1]
    rows = ts * grp
    cmp_per_sel = SEL_BLOCK // cmp_block
    lane = _lane((rows, nc))
    pos_r = past + lax.broadcasted_iota(jnp.int32, (rows, 1), 0) // grp
    valid = lane * cmp_block + (cmp_block - 1) <= pos_r
    imps = []
    for bb in range(nb):
        bc = bc_ref[bb]
        kc = bc[:, 0:LANES].astype(BF16)
        vc = bc[:, LANES:2 * LANES].astype(BF16)
        q = q_ref[bb]
        for kvh in range(2):
            p, o = _cmp_softmax(q[kvh * rows:(kvh + 1) * rows], kc, vc, valid)
            imps.append(jnp.sum(p.reshape(ts, grp, nc), axis=1))
            o_k = o if kvh == 0 else _swap64(o)
            o_ref[bb, kvh * rows:(kvh + 1) * rows, :] = o_k[:, 0:64] * g_ref[bb, kvh * rows:(kvh + 1) * rows, 0:1]
    imp = jnp.concatenate(imps, axis=0)
    n_rows = imp.shape[0]
    pos_t = past + lax.broadcasted_iota(jnp.int32, (n_rows, 1), 0) % ts
    picks = _select(imp, pos_t, n_sel, cmp_per_sel, cmp_block)
    lane_o = _lane((n_rows, LANES))
    idx = jnp.zeros((n_rows, LANES), jnp.int32)
    for it, first in enumerate(picks):
        idx = jnp.where(lane_o == it, first.astype(jnp.int32) // cmp_per_sel, idx)
    idx_ref[...] = idx.reshape(nb, 2 * ts, LANES)


def _cmp_sample(q_s, blocks_c, grow, *, ts, past, cmp_block, n_sel, grp, nb):
    bs, qrows = q_s.shape[0], q_s.shape[1]
    return pl.pallas_call(
        functools.partial(_cmp_sample_kernel, ts=ts, past=past, cmp_block=cmp_block, n_sel=n_sel, grp=grp),
        name="cmp_sample",
        grid=(bs // nb,),
        in_specs=[
            pl.BlockSpec((nb,) + q_s.shape[1:], lambda b: (b, 0, 0)),
            pl.BlockSpec((nb,) + blocks_c.shape[1:], lambda b: (b, 0, 0)),
            pl.BlockSpec((nb,) + grow.shape[1:], lambda b: (b, 0, 0)),
        ],
        out_specs=[
            pl.BlockSpec((nb, qrows, 64), lambda b: (b, 0, 0)),
            pl.BlockSpec((nb, 2 * ts, LANES), lambda b: (b, 0, 0)),
        ],
        out_shape=[jax.ShapeDtypeStruct((bs, qrows, 64), F32),
                   jax.ShapeDtypeStruct((bs, 2 * ts, LANES), jnp.int32)],
        compiler_params=_cparams(("arbitrary",)),
    )(q_s, blocks_c, grow)


def _selwin_sample_kernel(idx_ref, pt_ref, q_ref, win_ref, wnew_ref, g_ref, cache_ref, snew_ref,
                          osel_ref, owin_ref, buf, sem, *,
                          ts, grp, n_sel, past_blocks, per_page, n_pages):
    b = pl.program_id(0)
    nb = pl.num_programs(0)
    slot = b % 2

    def issue(bb, sl, wait):
        for kt in range(2 * ts):
            kvh = kt // ts
            for it in range(n_sel):
                blk = idx_ref[(bb * 2 * ts + kt) * n_sel + it]
                for s in range(2):
                    rows_s = pl.ds(s * LANES + kvh * 64, 64)
                    dst = buf.at[sl, kt, s, :, pl.ds(it * LANES, LANES)]
                    if wait:
                        pltpu.make_async_copy(snew_ref.at[bb, rows_s, :], dst, sem.at[sl]).wait()
                        continue
                    in_past = blk < past_blocks

                    @pl.when(in_past)
                    def _():
                        phys = pt_ref[bb * n_pages + blk // per_page]
                        pltpu.make_async_copy(cache_ref.at[phys, rows_s, :], dst, sem.at[sl]).start()

                    @pl.when(jnp.logical_not(in_past))
                    def _():
                        pltpu.make_async_copy(snew_ref.at[bb, rows_s, :], dst, sem.at[sl]).start()

    @pl.when(b == 0)
    def _():
        issue(0, 0, False)

    @pl.when(b + 1 < nb)
    def _():
        issue(b + 1, 1 - slot, False)

    issue(b, slot, True)

    qf = q_ref[0].astype(F32)
    half_rows = qf.shape[0] // 2
    q = jnp.concatenate([qf[0:half_rows, 0:64], _swap64(qf[half_rows:, :])[:, 0:64]], axis=0).astype(BF16)
    gates = g_ref[0]
    nkeys = n_sel * LANES
    lane_k = _lane((1, nkeys))
    within = lane_k % LANES
    rows = ts * grp
    for kvh in range(2):
        for t in range(ts):
            kt = kvh * ts + t
            kk = buf[slot, kt, 0].astype(BF16)
            vv = buf[slot, kt, 1].astype(BF16)
            lo = jnp.zeros((1, nkeys), jnp.int32)
            hi = jnp.zeros((1, nkeys), jnp.int32)
            for it in range(n_sel):
                blk = idx_ref[(b * 2 * ts + kt) * n_sel + it]
                in_past = blk < past_blocks
                half = (blk % per_page) * SEL_BLOCK
                here = (lane_k // LANES) == it
                lo = jnp.where(here, jnp.where(in_past, half, 0), lo)
                hi = jnp.where(here, jnp.where(in_past, half + SEL_BLOCK - 1, t), hi)
            kvalid = (within >= lo) & (within <= hi)
            r0 = kt * grp
            s = jnp.where(kvalid, _nn(q[r0:r0 + grp], kk), NEG_INF)
            m = jnp.max(s, axis=-1, keepdims=True)
            e = jnp.where(kvalid, jnp.exp(s - m), 0.0)
            l = jnp.sum(e, axis=-1, keepdims=True)
            p = e / jnp.where(l == 0.0, 1.0, l)
            osel_ref[0, r0:r0 + grp, :] = _nt(p.astype(BF16), vv) * gates[r0:r0 + grp, 1:2]

        r1 = kvh * rows
        wk = win_ref[0, kvh * 64:(kvh + 1) * 64, :].astype(BF16)
        wv = win_ref[0, LANES + kvh * 64:LANES + (kvh + 1) * 64, :].astype(BF16)
        nk_ = wnew_ref[0, kvh * 64:(kvh + 1) * 64, :].astype(BF16)
        nv_ = wnew_ref[0, LANES + kvh * 64:LANES + (kvh + 1) * 64, :].astype(BF16)
        wlen = wk.shape[1]
        qk = q[r1:r1 + rows]
        tok = lax.broadcasted_iota(jnp.int32, (rows, 1), 0) // grp
        v_old = _lane((rows, wlen)) > tok
        col_n = _lane((rows, LANES))
        v_new = (col_n <= tok) & (col_n < ts)
        s_old = jnp.where(v_old, _nn(qk, wk), NEG_INF)
        s_new = jnp.where(v_new, _nn(qk, nk_), NEG_INF)
        m = jnp.maximum(jnp.max(s_old, axis=-1, keepdims=True), jnp.max(s_new, axis=-1, keepdims=True))
        e_old = jnp.where(v_old, jnp.exp(s_old - m), 0.0)
        e_new = jnp.where(v_new, jnp.exp(s_new - m), 0.0)
        l = jnp.sum(e_old, axis=-1, keepdims=True) + jnp.sum(e_new, axis=-1, keepdims=True)
        inv = 1.0 / jnp.where(l == 0.0, 1.0, l)
        ow = _nt((e_old * inv).astype(BF16), wv) + _nt((e_new * inv).astype(BF16), nv_)
        owin_ref[0, r1:r1 + rows, :] = ow * gates[r1:r1 + rows, 2:3]


def _selwin_sample(idx_flat, page_table, q_c, win_t, wnew_t, grow, cache_sel_t, snew_t, *,
                   ts, grp, n_sel, past_blocks, per_page):
    bs, n_pages = page_table.shape
    qrows = q_c.shape[1]
    grid_spec = pltpu.PrefetchScalarGridSpec(
        num_scalar_prefetch=2, grid=(bs,),
        in_specs=[
            pl.BlockSpec((1,) + q_c.shape[1:], lambda b, i, p: (b, 0, 0)),
            pl.BlockSpec((1,) + win_t.shape[1:], lambda b, i, p: (b, 0, 0)),
            pl.BlockSpec((1,) + wnew_t.shape[1:], lambda b, i, p: (b, 0, 0)),
            pl.BlockSpec((1,) + grow.shape[1:], lambda b, i, p: (b, 0, 0)),
            pl.BlockSpec(memory_space=pl.ANY),
            pl.BlockSpec(memory_space=pl.ANY),
        ],
        out_specs=[pl.BlockSpec((1, qrows, 64), lambda b, i, p: (b, 0, 0)),
                   pl.BlockSpec((1, qrows, 64), lambda b, i, p: (b, 0, 0))],
        scratch_shapes=[pltpu.VMEM((2, 2 * ts, 2, 64, n_sel * LANES), F32), pltpu.SemaphoreType.DMA((2,))],
    )
    return pl.pallas_call(
        functools.partial(_selwin_sample_kernel, ts=ts, grp=grp, n_sel=n_sel, past_blocks=past_blocks,
                          per_page=per_page, n_pages=n_pages),
        name="selwin_sample", grid_spec=grid_spec,
        out_shape=[jax.ShapeDtypeStruct((bs, qrows, 64), F32)] * 2,
        compiler_params=_cparams(("arbitrary",)),
    )(idx_flat, page_table.reshape(-1), q_c, win_t, wnew_t, grow, cache_sel_t, snew_t)


def _outproj_kernel(h_ref, oa_ref, oc_ref, os_ref, ow_ref, wo_ref, gf_ref, wr_ref, br_ref,
                    h1_ref, xn_ref, e_ref, gw_ref, *, top_k):
    half = oa_ref.shape[1]
    ob = (oc_ref[...] + os_ref[...] + ow_ref[...]).astype(BF16)
    h1 = h_ref[...] + _nn(oa_ref[...], wo_ref[0:half, :]) + _nn(ob, wo_ref[half:2 * half, :])
    h1_ref[...] = h1
    xn = h1 * lax.rsqrt(jnp.mean(h1 * h1, axis=-1, keepdims=True) + RMS_EPS) * gf_ref[...]
    xn_ref[...] = xn
    hi, lo = _split(xn)
    w_hi, w_lo = _split(wr_ref[...])
    logits = _nn(hi, w_hi) + _nn(lo, w_hi) + _nn(hi, w_lo) + br_ref[...]
    lane = _lane(logits.shape)
    lane_f = lane.astype(F32)
    score = logits
    tops, es = [], []
    for _ in range(top_k):
        m = jnp.max(score, axis=-1, keepdims=True)
        first = jnp.min(jnp.where(score == m, lane_f, 1e9), axis=-1, keepdims=True)
        score = jnp.where(lane_f == first, REMOVED, score)
        tops.append(m)
        es.append(first)
    ws = [jnp.exp(t - tops[0]) for t in tops]
    tot = functools.reduce(lambda a, b2: a + b2, ws)
    e_out = jnp.zeros(logits.shape, jnp.int32)
    w_out = jnp.zeros(logits.shape, F32)
    for k in range(top_k):
        e_out = jnp.where(lane == k, es[k].astype(jnp.int32), e_out)
        w_out = jnp.where(lane == k, ws[k] / tot, w_out)
    e_ref[...] = e_out
    gw_ref[...] = w_out


def _outproj(h, oa, oc, os_, ow, wo, gf, wrt, br, *, tm, top_k):
    n, d = h.shape
    row = lambda i: (i, 0)
    full = lambda i: (0, 0)
    return pl.pallas_call(
        functools.partial(_outproj_kernel, top_k=top_k), name="outproj",
        grid=(n // tm,),
        in_specs=[pl.BlockSpec((tm, d), row), pl.BlockSpec((tm, oa.shape[1]), row),
                  pl.BlockSpec((tm, oc.shape[1]), row), pl.BlockSpec((tm, oc.shape[1]), row),
                  pl.BlockSpec((tm, oc.shape[1]), row),
                  pl.BlockSpec(wo.shape, full), pl.BlockSpec(gf.shape, full),
                  pl.BlockSpec(wrt.shape, full), pl.BlockSpec(br.shape, full)],
        out_specs=[pl.BlockSpec((tm, d), row), pl.BlockSpec((tm, d), row),
                   pl.BlockSpec((tm, LANES), row), pl.BlockSpec((tm, LANES), row)],
        out_shape=[jax.ShapeDtypeStruct((n, d), F32), jax.ShapeDtypeStruct((n, d), F32),
                   jax.ShapeDtypeStruct((n, LANES), jnp.int32), jax.ShapeDtypeStruct((n, LANES), F32)],
        compiler_params=_cparams(("parallel",)),
    )(h, oa, oc, os_, ow, wo, gf, wrt, br)


def _gather_kernel(idx_ref, src_ref, o_ref, sem, *, rows):
    g = pl.program_id(0)

    def start(r2, c):
        for pr in range(2):
            r = 2 * r2 + pr
            pltpu.make_async_copy(src_ref.at[pl.ds(idx_ref[g * rows + r], 1)], o_ref.at[pl.ds(r, 1)],
                                  sem.at[0]).start(priority=pr)
        return c

    lax.fori_loop(0, rows // 2, start, 0, unroll=4)
    pltpu.make_async_copy(src_ref.at[pl.ds(0, rows)], o_ref, sem.at[0]).wait()


def _gather_rows(idx, src, *, rows):
    n = idx.shape[0]
    d = src.shape[1]
    grid_spec = pltpu.PrefetchScalarGridSpec(
        num_scalar_prefetch=1, grid=(n // rows,),
        in_specs=[pl.BlockSpec(memory_space=pl.ANY)],
        out_specs=pl.BlockSpec((rows, d), lambda g, idx_: (g, 0)),
        scratch_shapes=[pltpu.SemaphoreType.DMA((1,))],
    )
    return pl.pallas_call(
        functools.partial(_gather_kernel, rows=rows), name="gather_rows",
        grid_spec=grid_spec,
        out_shape=jax.ShapeDtypeStruct((n, d), src.dtype),
        compiler_params=_cparams(("arbitrary",)),
    )(idx, src)


def _expert_kernel(be_ref, nu_ref, x_ref, w1_ref, b1_ref, w2_ref, b2_ref, o_ref, w1s, w2s, *, d_ff):
    i = pl.program_id(0)
    prev = be_ref[jnp.maximum(i - 1, 0)]

    @pl.when((i == 0) | (be_ref[i] != prev))
    def _():
        w1s[...] = w1_ref[0].astype(BF16)
        w2s[...] = w2_ref[0].astype(BF16)

    @pl.when(i < nu_ref[0])
    def _():
        hb = _nn(x_ref[...].astype(BF16), w1s[...]) + b1_ref[0]
        g = jnp.minimum(hb[:, 0:d_ff], SWIGLU_LIMIT)
        u = jnp.clip(hb[:, d_ff:2 * d_ff], -SWIGLU_LIMIT, SWIGLU_LIMIT)
        act = (u + 1.0) * g * jax.nn.sigmoid(SWIGLU_ALPHA * g)
        o_ref[...] = _nn(act.astype(BF16), w2s[...]) + b2_ref[0]

    @pl.when(i >= nu_ref[0])
    def _():
        o_ref[...] = jnp.zeros(o_ref.shape, F32)


def _experts(block_e, n_used, x_sorted, w1, b1, w2, b2, *, bm):
    r, d = x_sorted.shape
    d_ff = w2.shape[1]
    grid_spec = pltpu.PrefetchScalarGridSpec(
        num_scalar_prefetch=2, grid=(r // bm,),
        in_specs=[
            pl.BlockSpec((bm, d), lambda i, be, nu: (i, 0)),
            pl.BlockSpec((1,) + w1.shape[1:], lambda i, be, nu: (be[i], 0, 0)),
            pl.BlockSpec((1, 1, b1.shape[2]), lambda i, be, nu: (be[i], 0, 0)),
            pl.BlockSpec((1,) + w2.shape[1:], lambda i, be, nu: (be[i], 0, 0)),
            pl.BlockSpec((1, 1, b2.shape[2]), lambda i, be, nu: (be[i], 0, 0)),
        ],
        out_specs=pl.BlockSpec((bm, d), lambda i, be, nu: (i, 0)),
        scratch_shapes=[pltpu.VMEM(w1.shape[1:], BF16), pltpu.VMEM(w2.shape[1:], BF16)],
    )
    return pl.pallas_call(
        functools.partial(_expert_kernel, d_ff=d_ff), name="experts",
        grid_spec=grid_spec,
        out_shape=jax.ShapeDtypeStruct((r, d), F32),
        compiler_params=_cparams(("arbitrary",)),
    )(block_e, n_used, x_sorted, w1, b1, w2, b2)


def _final_kernel(h1_ref, y_ref, gw_ref, p_ref, wp_ref, gp_ref, wg_ref, gfin_ref, o_ref, *, top_k):
    h2 = h1_ref[...]
    gw = gw_ref[...]
    for k in range(top_k):
        h2 = h2 + y_ref[k] * gw[:, k:k + 1]
    hn = h2 * lax.rsqrt(jnp.mean(h2 * h2, axis=-1, keepdims=True) + RMS_EPS) * gp_ref[...]
    gate = jax.nn.sigmoid(_nn(hn.astype(BF16), wg_ref[...]))
    h3 = h2 + _nn(p_ref[...].astype(BF16), wp_ref[...]) * gate
    o_ref[...] = h3 * lax.rsqrt(jnp.mean(h3 * h3, axis=-1, keepdims=True) + RMS_EPS) * gfin_ref[...]


def _final(h1, y4, gate_w, p, wp, gp, wg, gfin, *, tm, top_k):
    n, d = h1.shape
    row = lambda i: (i, 0)
    full = lambda i: (0, 0)
    return pl.pallas_call(
        functools.partial(_final_kernel, top_k=top_k), name="final",
        grid=(n // tm,),
        in_specs=[pl.BlockSpec((tm, d), row), pl.BlockSpec((top_k, tm, d), lambda i: (0, i, 0)),
                  pl.BlockSpec((tm, LANES), row), pl.BlockSpec((tm, p.shape[1]), row),
                  pl.BlockSpec(wp.shape, full), pl.BlockSpec(gp.shape, full), pl.BlockSpec(wg.shape, full),
                  pl.BlockSpec(gfin.shape, full)],
        out_specs=pl.BlockSpec((tm, d), row),
        out_shape=jax.ShapeDtypeStruct((n, d), F32),
        compiler_params=_cparams(("parallel",)),
    )(h1, y4, gate_w, p, wp, gp, wg, gfin)


def _rope_tables(pos, dim, reps, width):
    inv = ROPE_THETA ** (-jnp.arange(0, dim, 2, dtype=F32) / dim)
    ang = pos.astype(F32)[:, None] * inv[None, :]
    cos, sin = jnp.cos(ang), jnp.sin(ang)
    c = jnp.tile(jnp.concatenate([cos, cos], axis=1), (1, reps))
    s = jnp.tile(jnp.concatenate([-sin, sin], axis=1), (1, reps))
    pad = width - c.shape[1]
    return jnp.pad(c, ((0, 0), (0, pad))), jnp.pad(s, ((0, 0), (0, pad)))


def _tile_rows(n, pref):
    for t in (pref, 256, 128, 64, 32, 16, 8):
        if t <= pref and n % t == 0:
            return t
    return n


def kernel(x_prompt, x_sample, cache_mla, cache_nsa_cmp, cache_nsa_sel, state_nsa_win, page_table, p_prompt, p_sample,
           g_attn, w_in, g_q, w_uq, g_kv, w_uk, w_uv, cmp_pe, w_cmp1, w_cmp2, w_o, g_ffn, w_router, b_router,
           w_e_in, b_e_in, w_e_out, b_e_out, g_ple, w_ple, w_ple_gate, g_final):
    depth = g_attn.shape[0]
    assert depth == 1
    b, t, d = x_prompt.shape
    bs, ts, _ = x_sample.shape
    n_pages = page_table.shape[1]
    page_rows = cache_mla.shape[2]
    past = n_pages * page_rows
    lat = cache_mla.shape[3]
    q_rank, n_heads, qk_dim = w_uq.shape[1:]
    kv_rank, _, nope = w_uk.shape[1:]
    rope_a = qk_dim - nope
    v_dim = w_uv.shape[3]
    cmp_block = cmp_pe.shape[1]
    kvh, hd = cache_nsa_cmp.shape[4], cache_nsa_cmp.shape[5]
    kv_cols = 2 * kvh * hd
    n_q = 8
    grp = n_q // kvh
    window = 512
    n_exp = w_router.shape[2]
    d_ff = w_e_out.shape[2]
    assert (kvh, hd, rope_a, kv_rank, q_rank, kv_cols) == (2, 64, 32, 128, 256, 256)
    np_, ns_ = b * t, bs * ts
    n = np_ + ns_
    mla_scale = float(qk_dim) ** -0.5
    nsa_scale = float(hd) ** -0.5

    wi = w_in[0]
    o_cq, o_ckv, o_kr, o_qb, o_cmp, o_sel, o_win, o_gate = 0, 256, 384, 416, 928, 1184, 1440, 1696
    misc_w = jnp.concatenate([wi[:, o_kr:o_kr + 32], wi[:, o_gate:o_gate + 24], jnp.zeros((d, 72), F32)], axis=1)
    win_r = jnp.concatenate([wi[:, o_cq:o_cq + 256], wi[:, o_ckv:o_ckv + 128], wi[:, o_qb:o_qb + 512],
                             wi[:, o_cmp:o_cmp + 256], wi[:, o_sel:o_sel + 256], wi[:, o_win:o_win + 256],
                             misc_w], axis=1).astype(BF16)
    wnope = jnp.transpose(w_uq[0][:, :, :nope], (1, 0, 2))
    wuk = jnp.transpose(w_uk[0], (1, 0, 2))
    wrope = jnp.transpose(w_uq[0][:, :, nope:], (1, 0, 2))
    wrope_p = jnp.concatenate([wrope[:, :, rope_a // 2:], wrope[:, :, :rope_a // 2]], axis=2)
    pad_r = lambda w: jnp.transpose(jnp.pad(w, ((0, 0), (0, 0), (0, LANES - rope_a))), (1, 0, 2)).reshape(
        q_rank, n_heads * LANES).astype(BF16)
    wr, wrp = pad_r(wrope), pad_r(wrope_p)
    wuv_t = jnp.transpose(w_uv[0], (1, 0, 2))
    zv = jnp.zeros_like(wuv_t)
    odd = (jnp.arange(n_heads) % 2 == 1)[:, None, None]
    wuv_h = jnp.concatenate([jnp.where(odd, zv, wuv_t), jnp.where(odd, wuv_t, zv)], axis=2).astype(BF16)
    wuv_flat = w_uv[0].reshape(kv_rank, n_heads * v_dim).astype(BF16)

    pe = cmp_pe[0]
    pe_full = jnp.broadcast_to(pe[:, :, None, :], (cmp_block, 2, kvh, hd)).reshape(cmp_block, kv_cols)
    w1 = w_cmp1[0].reshape(2, cmp_block, hd, -1)
    hid = w1.shape[3]
    eye = jnp.eye(kvh, dtype=F32)
    w1bd = jnp.einsum('srdh,kK->srkdKh', w1, eye).reshape(2, cmp_block, kvh * hd, kvh * hid).astype(BF16)
    w2bd = jnp.einsum('shd,kK->skhKd', w_cmp2[0], eye).reshape(2, kvh * hid, kvh * hd).astype(BF16)

    wo = w_o[0].astype(BF16)
    wrt = jnp.pad(w_router[0], ((0, 0), (0, LANES - n_exp)))
    br = jnp.concatenate([b_router[0], jnp.full((LANES - n_exp,), NEG_INF, F32)])[None, :]

    pos = jnp.concatenate([jnp.tile(jnp.arange(t, dtype=jnp.int32), b),
                           jnp.tile(past + jnp.arange(ts, dtype=jnp.int32), bs)])
    cosa, sina = _rope_tables(pos, rope_a, 1, LANES)
    cosb, sinb = _rope_tables(pos, hd, 2, LANES)

    x = jnp.concatenate([x_prompt.reshape(np_, d), x_sample.reshape(ns_, d)], axis=0)
    tm = _tile_rows(n, 256)
    (rows_a, kmla, qmla, rows_c, rows_s, rows_w, kvs, kvw, qraw, qrope, gates) = _proj(
        x, cosa, sina, cosb, sinb, g_attn, win_r, g_q, g_kv, wnope, wuk, wr, wrp,
        tm=tm, mla_scale=mla_scale, nsa_scale=nsa_scale)

    tq_a = _tile_rows(t, 128)
    tk_a = _tile_rows(t, 512)
    oa_p = _mla_prompt(qmla, kmla, wuv_h, b=b, t=t, tq=tq_a, tk=tk_a)

    pages_c = 32
    cpages = np_ // page_rows
    rows_c_pages = rows_c[:np_].reshape(cpages, page_rows, kv_cols)
    blocks_p = _compress(jnp.arange(cpages, dtype=jnp.int32), rows_c_pages, pe_full, w1bd, w2bd,
                         pages=min(pages_c, cpages), transposed=False)
    tq_b = _tile_rows(t, 256)
    oc_p, selmask = _cmp_prompt(qraw, blocks_p, gates, b=b, t=t, tq=tq_b, cmp_block=cmp_block)
    os_p = _sel_prompt(qrope, kvs, selmask, gates, b=b, t=t, tq=tq_b, tk=_tile_rows(t, 512), cmp_block=cmp_block)
    ow_p = _win_prompt(qrope, kvw, gates, b=b, t=t, tq=tq_b, window=window)

    q_s = jnp.transpose(qmla[:, np_:, :].reshape(n_heads, bs, ts, 256), (1, 0, 2, 3)).reshape(bs, n_heads * ts, 256)
    knew = jnp.pad(kmla[np_:].reshape(bs, ts, 256), ((0, 0), (0, 16 - ts), (0, 0)))
    rows_last = lambda c: jnp.transpose(c, (0, 2, 3, 4, 1)).reshape(c.shape[0], kv_cols, c.shape[1])
    cache_mla_t = jnp.transpose(cache_mla[0], (0, 2, 1))
    cache_cmp_t = rows_last(cache_nsa_cmp[0])
    cache_sel_t = rows_last(cache_nsa_sel[0])
    win_t = rows_last(state_nsa_win[0])
    pages_a = min(32, n_pages)
    oa_s = _mla_sample(page_table, q_s, knew, wuv_flat, cache_mla_t, pages=pages_a, ts=ts).reshape(ns_, -1)

    sum_past = _compress(page_table.reshape(-1), cache_cmp_t, pe_full, w1bd, w2bd,
                         pages=min(pages_c, n_pages), transposed=True)
    per_page_c = page_rows // cmp_block
    new_len = -(-ts // SEL_BLOCK) * SEL_BLOCK
    new_pages = -(-new_len // page_rows)
    rc_new = jnp.pad(rows_c[np_:].reshape(bs, ts, kv_cols), ((0, 0), (0, new_pages * page_rows - ts), (0, 0)))
    sum_new = _compress(jnp.arange(bs * new_pages, dtype=jnp.int32), rc_new.reshape(bs * new_pages, page_rows, kv_cols),
                        pe_full, w1bd, w2bd, pages=min(pages_c, bs * new_pages), transposed=False)
    nc_s = past // cmp_block + new_len // cmp_block
    nc_pad = -(-nc_s // LANES) * LANES
    blocks_s = jnp.concatenate([
        sum_past.reshape(bs, n_pages * per_page_c, kv_cols),
        sum_new.reshape(bs, new_pages * per_page_c, kv_cols)[:, :new_len // cmp_block],
        jnp.zeros((bs, nc_pad - nc_s, kv_cols), F32)], axis=1)
    to_s = lambda q: jnp.transpose(q[:, np_:, :].reshape(kvh, grp, bs, ts, LANES), (2, 0, 3, 1, 4)).reshape(
        bs, kvh * ts * grp, LANES)
    grow = jnp.transpose(gates[np_:, 32:32 + 3 * n_q].reshape(bs, ts, kvh, grp, 3), (0, 2, 1, 3, 4)).reshape(
        bs, kvh * ts * grp, 3)
    grow = jnp.pad(grow, ((0, 0), (0, 0), (0, LANES - 3)))
    n_sel = min(N_SEL, nc_s * cmp_block // SEL_BLOCK)
    nb_c = 8 if bs % 8 == 0 else 1
    oc_s, idx = _cmp_sample(to_s(qraw), blocks_s, grow, ts=ts, past=past, cmp_block=cmp_block, n_sel=n_sel, grp=grp,
                            nb=nb_c)
    idx_flat = idx[:, :, :n_sel].reshape(-1)
    new_t = lambda r: jnp.pad(jnp.transpose(r[np_:].reshape(bs, ts, kv_cols), (0, 2, 1)),
                              ((0, 0), (0, 0), (0, page_rows - ts)))
    os_s, ow_s = _selwin_sample(idx_flat, page_table, to_s(qrope), win_t, new_t(rows_w), grow, cache_sel_t,
                                new_t(rows_s),
                                ts=ts, grp=grp, n_sel=n_sel, past_blocks=past // SEL_BLOCK,
                                per_page=page_rows // SEL_BLOCK)
    from_s = lambda o: jnp.transpose(o.reshape(bs, kvh, ts, grp, 64), (0, 2, 1, 3, 4)).reshape(ns_, n_q * 64)

    cat = lambda p_, s_: jnp.concatenate([p_, s_.reshape(ns_, -1).astype(p_.dtype)], axis=0)
    h1, xn, top_e, gate_w = _outproj(x, cat(oa_p, oa_s), cat(oc_p, from_s(oc_s)), cat(os_p, from_s(os_s)),
                                     cat(ow_p, from_s(ow_s)), wo, g_ffn, wrt, br, tm=tm, top_k=TOP_K)

    bm = 256
    flat_e = top_e[:, :TOP_K].reshape(-1)
    n_assign = flat_e.shape[0]
    onehot = (flat_e[:, None] == jnp.arange(n_exp, dtype=jnp.int32)[None, :]).astype(jnp.int32)
    rank = jnp.take_along_axis(jnp.cumsum(onehot, axis=0), flat_e[:, None], axis=1)[:, 0] - 1
    counts = jnp.sum(onehot, axis=0)
    padded = (counts + bm - 1) // bm * bm
    pad_end = jnp.cumsum(padded)
    dest = (pad_end - padded)[flat_e] + rank
    n_blocks = -(-n_assign // bm) + n_exp
    row_tok = jnp.zeros((n_blocks * bm,), jnp.int32).at[dest].set(jnp.arange(n_assign, dtype=jnp.int32) // TOP_K)
    block_start = jnp.arange(n_blocks, dtype=jnp.int32) * bm
    block_e = jnp.minimum(jnp.sum((pad_end[None, :] <= block_start[:, None]).astype(jnp.int32), axis=1), n_exp - 1)
    n_used = (pad_end[-1] // bm).astype(jnp.int32)[None]

    x_sorted = _gather_rows(row_tok, xn, rows=bm)
    y_sorted = _experts(block_e, n_used, x_sorted, w_e_in[0], b_e_in[0][:, None, :],
                        w_e_out[0], b_e_out[0][:, None, :], bm=bm)
    dest_k = jnp.transpose(dest.reshape(n, TOP_K)).reshape(-1).astype(jnp.int32)
    y4 = _gather_rows(dest_k, y_sorted, rows=_tile_rows(n, 256)).reshape(TOP_K, n, d)

    p_all = jnp.concatenate([p_prompt[0].reshape(np_, -1), p_sample[0].reshape(ns_, -1)], axis=0)
    y = _final(h1, y4, gate_w, p_all, w_ple[0].astype(BF16), g_ple, w_ple_gate[0].astype(BF16), g_final[None, :],
               tm=tm, top_k=TOP_K)

    kv_shape = (2, kvh, hd)
    wlen = state_nsa_win.shape[2]
    rows_w_p = rows_w[:np_].reshape(b, t, kv_cols)
    win_s = jnp.concatenate([state_nsa_win[0].reshape(bs, wlen, kv_cols), rows_w[np_:].reshape(bs, ts, kv_cols)], axis=1)[:, ts:]
    return (
        y[:np_].reshape(b, t, d), y[np_:].reshape(bs, ts, d),
        rows_a[:np_].reshape(1, b, t, lat), rows_a[np_:].reshape(1, bs, ts, lat),
        rows_c[:np_].reshape((1, b, t) + kv_shape), rows_c[np_:].reshape((1, bs, ts) + kv_shape),
        rows_s[:np_].reshape((1, b, t) + kv_shape), rows_s[np_:].reshape((1, bs, ts) + kv_shape),
        rows_w_p[:, t - min(window, t):].reshape((1, b, min(window, t)) + kv_shape),
        win_s.reshape((1, bs, wlen) + kv_shape),
    )
```

```python
import functools

import jax
import jax.numpy as jnp
from jax import lax
from jax.experimental import pallas as pl
from jax.experimental.pallas import tpu as pltpu

RMS_EPS = 1e-6
ROPE_THETA = 10000.0
NEG_INF = -1e30
REMOVED = -3e38
FORCE_SCORE = 1e4
SEL_BLOCK = 64
N_SEL = 16
TOP_K = 4
SWIGLU_LIMIT = 7.0
SWIGLU_ALPHA = 1.702

LANES = 128
VMEM_LIMIT = 56 * 1024 * 1024

F32 = jnp.float32
BF16 = jnp.bfloat16


def _cparams(sem):
    return pltpu.CompilerParams(dimension_semantics=sem, vmem_limit_bytes=VMEM_LIMIT)


def _nt(a, b):
    return lax.dot_general(a, b, (((1,), (1,)), ((), ())), preferred_element_type=F32)


def _nn(a, b):
    return jnp.dot(a, b, preferred_element_type=F32)


def _split(x):
    hi = x.astype(BF16)
    lo = (x - hi.astype(F32)).astype(BF16)
    return hi, lo


def _lane(shape, dim=None):
    return lax.broadcasted_iota(jnp.int32, shape, len(shape) - 1 if dim is None else dim)


def _partner(x, half):
    w = x.shape[-1]
    lane = _lane(x.shape)
    first = (lane % (2 * half)) < half
    return jnp.where(first, pltpu.roll(x, w - half, x.ndim - 1), pltpu.roll(x, half, x.ndim - 1))


def _swap64(x):
    return pltpu.roll(x, 64, x.ndim - 1)


def _proj_kernel(x_ref, ga_ref, win_ref, gq_ref, gkv_ref, wnope_ref, wuk_ref, wr_ref, wrp_ref,
                 cosa_ref, sina_ref, cosb_ref, sinb_ref,
                 rowsa_ref, kmla_ref, qmla_ref, rowsc_ref, rowss_ref, rowsw_ref,
                 kvs_ref, kvw_ref, qraw_ref, qrope_ref, gates_ref,
                 wabs_ref, *, mla_scale, nsa_scale):
    n_heads = qmla_ref.shape[0]

    @pl.when(pl.program_id(0) == 0)
    def _():
        for h in range(n_heads):
            a_hi, a_lo = _split(wnope_ref[h])
            b_hi, b_lo = _split(wuk_ref[h])
            w = _nt(a_hi, b_hi) + _nt(a_lo, b_hi) + _nt(a_hi, b_lo)
            wabs_ref[:, h * LANES:(h + 1) * LANES] = w.astype(BF16)

    x = x_ref[...]
    xn = x * lax.rsqrt(jnp.mean(x * x, axis=-1, keepdims=True) + RMS_EPS) * ga_ref[...]
    proj = _nn(xn.astype(BF16), win_ref[...])
    cq = proj[:, 0:256]
    ckv = proj[:, 256:384]
    qb = proj[:, 384:896]
    cmp_raw = proj[:, 896:1152]
    sel_raw = proj[:, 1152:1408]
    win_raw = proj[:, 1408:1664]
    misc = proj[:, 1664:1792]

    cosa, sina = cosa_ref[...], sina_ref[...]
    cosb, sinb = cosb_ref[...], sinb_ref[...]

    cqn = (cq * lax.rsqrt(jnp.mean(cq * cq, axis=-1, keepdims=True) + RMS_EPS) * gq_ref[...]).astype(BF16)
    ckvn = ckv * lax.rsqrt(jnp.mean(ckv * ckv, axis=-1, keepdims=True) + RMS_EPS) * gkv_ref[...]
    krope = misc * cosa + _partner(misc, 16) * sina
    rowsa_ref[:, 0:128] = ckvn
    rowsa_ref[:, 128:160] = krope[:, 0:32]
    kmla_ref[...] = jnp.concatenate([ckvn, krope], axis=1).astype(BF16)
    qlat = _nn(cqn, wabs_ref[...])
    qr = _nn(cqn, wr_ref[...])
    qrp = _nn(cqn, wrp_ref[...])
    for h in range(n_heads):
        sl = slice(h * LANES, (h + 1) * LANES)
        rope_h = qr[:, sl] * cosa + qrp[:, sl] * sina
        qmla_ref[h] = (jnp.concatenate([qlat[:, sl], rope_h], axis=1) * mla_scale).astype(BF16)

    rowsc_ref[...] = cmp_raw
    sel_k = sel_raw[:, 0:128] * cosb + _partner(sel_raw[:, 0:128], 32) * sinb
    win_k = win_raw[:, 0:128] * cosb + _partner(win_raw[:, 0:128], 32) * sinb
    rows_s = jnp.concatenate([sel_k, sel_raw[:, 128:256]], axis=1)
    rows_w = jnp.concatenate([win_k, win_raw[:, 128:256]], axis=1)
    rowss_ref[...] = rows_s
    rowsw_ref[...] = rows_w
    kvs_ref[...] = rows_s.astype(BF16)
    kvw_ref[...] = rows_w.astype(BF16)
    cos4 = jnp.concatenate([cosb] * 4, axis=1)
    sin4 = jnp.concatenate([sinb] * 4, axis=1)
    qb_rope = qb * cos4 + _partner(qb, 32) * sin4
    lane = _lane((x.shape[0], LANES))
    n_q = qraw_ref.shape[0]
    for hh in range(n_q):
        kvh = hh // (n_q // 2)
        blk = slice((hh // 2) * LANES, (hh // 2 + 1) * LANES)
        keep = (lane >= 64) if kvh == 1 else (lane < 64)
        for src, dst in ((qb, qraw_ref), (qb_rope, qrope_ref)):
            v = src[:, blk]
            if (hh % 2) != kvh:
                v = _swap64(v)
            dst[hh] = jnp.where(keep, v * nsa_scale, 0.0).astype(BF16)
    gates_ref[...] = jax.nn.sigmoid(misc)


def _proj(x, cosa, sina, cosb, sinb, ga, win_r, gq, gkv, wnope, wuk, wr, wrp, *, tm, mla_scale, nsa_scale):
    n, d = x.shape
    n_heads = wnope.shape[0]
    grid = (n // tm,)
    row = lambda i: (i, 0)
    full2 = lambda i: (0, 0)
    full3 = lambda i: (0, 0, 0)
    head_row = lambda i: (0, i, 0)
    in_specs = [
        pl.BlockSpec((tm, d), row),
        pl.BlockSpec(ga.shape, full2),
        pl.BlockSpec(win_r.shape, full2),
        pl.BlockSpec(gq.shape, full2),
        pl.BlockSpec(gkv.shape, full2),
        pl.BlockSpec(wnope.shape, full3),
        pl.BlockSpec(wuk.shape, full3),
        pl.BlockSpec(wr.shape, full2),
        pl.BlockSpec(wrp.shape, full2),
        pl.BlockSpec((tm, LANES), row), pl.BlockSpec((tm, LANES), row),
        pl.BlockSpec((tm, LANES), row), pl.BlockSpec((tm, LANES), row),
    ]
    out_shape = [
        jax.ShapeDtypeStruct((n, 160), F32),
        jax.ShapeDtypeStruct((n, 256), BF16),
        jax.ShapeDtypeStruct((n_heads, n, 256), BF16),
        jax.ShapeDtypeStruct((n, 256), F32),
        jax.ShapeDtypeStruct((n, 256), F32),
        jax.ShapeDtypeStruct((n, 256), F32),
        jax.ShapeDtypeStruct((n, 256), BF16),
        jax.ShapeDtypeStruct((n, 256), BF16),
        jax.ShapeDtypeStruct((8, n, LANES), BF16),
        jax.ShapeDtypeStruct((8, n, LANES), BF16),
        jax.ShapeDtypeStruct((n, LANES), F32),
    ]
    out_specs = [
        pl.BlockSpec((tm, 160), row),
        pl.BlockSpec((tm, 256), row),
        pl.BlockSpec((n_heads, tm, 256), head_row),
        pl.BlockSpec((tm, 256), row),
        pl.BlockSpec((tm, 256), row),
        pl.BlockSpec((tm, 256), row),
        pl.BlockSpec((tm, 256), row),
        pl.BlockSpec((tm, 256), row),
        pl.BlockSpec((8, tm, LANES), head_row),
        pl.BlockSpec((8, tm, LANES), head_row),
        pl.BlockSpec((tm, LANES), row),
    ]
    return pl.pallas_call(
        functools.partial(_proj_kernel, mla_scale=mla_scale, nsa_scale=nsa_scale), name="proj",
        grid=grid, in_specs=in_specs, out_specs=out_specs, out_shape=out_shape,
        scratch_shapes=[pltpu.VMEM((wnope.shape[1], n_heads * LANES), BF16)],
        compiler_params=_cparams(("arbitrary",)),
    )(x, ga, win_r, gq, gkv, wnope, wuk, wr, wrp, cosa, sina, cosb, sinb)


def _mla_prompt_kernel(q_ref, k_ref, wuv_ref, o_ref, m_scr, l_scr, acc_scr, *, tq, tk):
    i, j = pl.program_id(1), pl.program_id(2)
    nh = q_ref.shape[0]
    jmax = (i * tq + tq - 1) // tk

    @pl.when(j == 0)
    def _():
        m_scr[...] = jnp.full(m_scr.shape, NEG_INF, F32)
        l_scr[...] = jnp.zeros(l_scr.shape, F32)
        acc_scr[...] = jnp.zeros(acc_scr.shape, F32)

    def step(masked):
        q = q_ref[...].reshape(nh * tq, q_ref.shape[2])
        k = k_ref[...]
        s = _nt(q, k)
        if masked:
            qpos = i * tq + lax.broadcasted_iota(jnp.int32, (1, tq, tk), 1)
            kpos = j * tk + lax.broadcasted_iota(jnp.int32, (1, tq, tk), 2)
            s = jnp.where(kpos <= qpos, s.reshape(nh, tq, tk), NEG_INF).reshape(nh * tq, tk)
        m_old = m_scr[...]
        m_new = jnp.maximum(m_old, jnp.max(s, axis=-1, keepdims=True))
        alpha = jnp.exp(m_old - m_new)
        p = jnp.exp(s - m_new)
        l_scr[...] = l_scr[...] * alpha + jnp.sum(p, axis=-1, keepdims=True)
        acc_scr[...] = acc_scr[...] * alpha + _nn(p.astype(BF16), k[:, 0:LANES])
        m_scr[...] = m_new

    full = (j + 1) * tk - 1 <= i * tq
    pl.when(full)(lambda: step(False))
    pl.when(jnp.logical_and(j <= jmax, jnp.logical_not(full)))(lambda: step(True))

    @pl.when(j == pl.num_programs(2) - 1)
    def _():
        o = (acc_scr[...] / l_scr[...]).astype(BF16).reshape(nh, tq, LANES)
        for m in range(nh // 2):
            pair = _nn(o[2 * m], wuv_ref[2 * m]) + _nn(o[2 * m + 1], wuv_ref[2 * m + 1])
            o_ref[:, m * LANES:(m + 1) * LANES] = pair.astype(o_ref.dtype)


def _mla_prompt(qmla, kmla, wuv, *, b, t, tq, tk):
    nh = qmla.shape[0]
    nq, nk = t // tq, t // tk
    out_w = nh * wuv.shape[2] // 2

    def kmap(bb, i, j):
        return (bb * nk + jnp.minimum(j, (i * tq + tq - 1) // tk), 0)

    return pl.pallas_call(
        functools.partial(_mla_prompt_kernel, tq=tq, tk=tk), name="mla_prompt",
        grid=(b, nq, nk),
        in_specs=[
            pl.BlockSpec((nh, tq, qmla.shape[2]), lambda bb, i, j: (0, bb * nq + i, 0)),
            pl.BlockSpec((tk, kmla.shape[1]), kmap),
            pl.BlockSpec(wuv.shape, lambda bb, i, j: (0, 0, 0)),
        ],
        out_specs=pl.BlockSpec((tq, out_w), lambda bb, i, j: (bb * nq + i, 0)),
        out_shape=jax.ShapeDtypeStruct((b * t, out_w), BF16),
        scratch_shapes=[pltpu.VMEM((nh * tq, 1), F32), pltpu.VMEM((nh * tq, 1), F32),
                        pltpu.VMEM((nh * tq, LANES), F32)],
        compiler_params=_cparams(("arbitrary", "arbitrary", "arbitrary")),
    )(qmla, kmla, wuv)


def _page_pipeline(g, copies):
    n = pl.num_programs(0)
    slot = g % 2

    @pl.when(g == 0)
    def _():
        for c in copies(0, 0):
            c.start()

    @pl.when(g + 1 < n)
    def _():
        for c in copies(g + 1, 1 - slot):
            c.start()

    for c in copies(g, slot):
        c.wait()
    return slot


def _compress_contract(load, w1_ref, w2_ref, o_ref, nblk, cmp_block):
    for s in range(2):
        acc = jnp.zeros((nblk, w1_ref.shape[3]), F32)
        for r in range(cmp_block):
            acc = acc + _nn(load(s, r), w1_ref[s, r])
        hid = jax.nn.gelu(acc)
        o_ref[:, s * LANES:(s + 1) * LANES] = _nn(hid.astype(BF16), w2_ref[s])


def _compress_kernel(tab_ref, src_ref, pe_ref, w1_ref, w2_ref, o_ref, buf, sem, *, pages, page_rows, cmp_block):
    g = pl.program_id(0)

    def copies(step, sl):
        return [pltpu.make_async_copy(src_ref.at[tab_ref[step * pages + p], :, pl.ds(s * LANES, LANES)],
                                      buf.at[sl, s, pl.ds(p * page_rows, page_rows), :], sem.at[sl])
                for p in range(pages) for s in range(2)]

    slot = _page_pipeline(g, copies)
    nblk = pages * page_rows // cmp_block

    def load(s, r):
        xr = buf[slot, s, pl.ds(r, nblk, stride=cmp_block), :] + pe_ref[r:r + 1, s * LANES:(s + 1) * LANES]
        return xr.astype(BF16)

    _compress_contract(load, w1_ref, w2_ref, o_ref, nblk, cmp_block)


def _compress_t_kernel(tab_ref, src_ref, pet_ref, perm_ref, w1_ref, w2_ref, o_ref, buf, sem, xr, *,
                       pages, page_rows, cmp_block):
    g = pl.program_id(0)

    def copies(step, sl):
        return [pltpu.make_async_copy(src_ref.at[tab_ref[step * pages + p]], buf.at[sl, p], sem.at[sl])
                for p in range(pages)]

    slot = _page_pipeline(g, copies)
    per_pair = 2 * page_rows // cmp_block
    perm = perm_ref[...]
    pet = pet_ref[...]
    for q in range(pages // 2):
        xt = jnp.concatenate([buf[slot, 2 * q], buf[slot, 2 * q + 1]], axis=1)
        xb = (xt + pet).astype(BF16)
        rows = _nt(perm, xb)
        xr[:, q * per_pair:(q + 1) * per_pair, :] = rows.reshape(cmp_block, per_pair, rows.shape[1])
    nblk = pages * page_rows // cmp_block
    load = lambda s, r: xr[r, :, s * LANES:(s + 1) * LANES].astype(BF16)
    _compress_contract(load, w1_ref, w2_ref, o_ref, nblk, cmp_block)


def _compress(table, src_pages, pe_full, w1bd, w2bd, *, pages, transposed):
    n_pages = table.shape[0]
    cmp_block, width = pe_full.shape
    page_rows = src_pages.shape[2] if transposed else src_pages.shape[1]
    per_page = page_rows // cmp_block
    const = []
    if transposed:
        assert page_rows == LANES and pages % 2 == 0 and 2 * per_page == 8
        pet = jnp.tile(jnp.transpose(pe_full), (1, 2 * per_page))
        src_lane = jnp.arange(2 * page_rows)
        dst_row = (src_lane % cmp_block) * (2 * per_page) + src_lane // cmp_block
        perm = (jnp.arange(2 * page_rows)[:, None] == dst_row[None, :]).astype(BF16)
        const = [pet, perm]
        body = _compress_t_kernel
        scratch = [pltpu.VMEM((2, pages, width, page_rows), F32), pltpu.SemaphoreType.DMA((2,)),
                   pltpu.VMEM((cmp_block, pages * per_page, width), F32)]
    else:
        const = [pe_full]
        body = _compress_kernel
        scratch = [pltpu.VMEM((2, 2, pages * page_rows, LANES), F32), pltpu.SemaphoreType.DMA((2,))]
    grid_spec = pltpu.PrefetchScalarGridSpec(
        num_scalar_prefetch=1, grid=(n_pages // pages,),
        in_specs=[pl.BlockSpec(memory_space=pl.ANY)]
                 + [pl.BlockSpec(c.shape, lambda g, tab: (0, 0)) for c in const]
                 + [pl.BlockSpec(w1bd.shape, lambda g, tab: (0, 0, 0, 0)),
                    pl.BlockSpec(w2bd.shape, lambda g, tab: (0, 0, 0))],
        out_specs=pl.BlockSpec((pages * per_page, width), lambda g, tab: (g, 0)),
        scratch_shapes=scratch,
    )
    return pl.pallas_call(
        functools.partial(body, pages=pages, page_rows=page_rows, cmp_block=cmp_block),
        name="compress_t" if transposed else "compress",
        grid_spec=grid_spec,
        out_shape=jax.ShapeDtypeStruct((n_pages * per_page, width), F32),
        compiler_params=_cparams(("arbitrary",)),
    )(table, src_pages, *const, w1bd, w2bd)


def _cmp_softmax(q, kc, vc, valid):
    s = jnp.where(valid, _nt(q, kc), NEG_INF)
    m = jnp.max(s, axis=-1, keepdims=True)
    e = jnp.where(valid, jnp.exp(s - m), 0.0)
    l = jnp.sum(e, axis=-1, keepdims=True)
    p = e / jnp.where(l == 0.0, 1.0, l)
    return p, _nn(p.astype(BF16), vc)


def _select(imp, pos, n_sel, cmp_per_sel, cmp_block):
    nc = imp.shape[-1]
    lane = _lane(imp.shape)
    pair = imp
    for d in range(1, cmp_per_sel):
        pair = pair + pltpu.roll(imp, nc - d, imp.ndim - 1)
    j = lane // cmp_per_sel
    qb = pos // (cmp_per_sel * cmp_block)
    forced = (j == 0) | (j == qb) | (j == qb - 1)
    causal = j * (cmp_per_sel * cmp_block) <= pos
    score = jnp.where(causal, jnp.where(forced, FORCE_SCORE, pair), NEG_INF)
    score = jnp.where(lane % cmp_per_sel == 0, score, REMOVED)
    lane_f = lane.astype(F32)
    picks = []
    for _ in range(n_sel):
        m = jnp.max(score, axis=-1, keepdims=True)
        first = jnp.min(jnp.where(score == m, lane_f, 1e9), axis=-1, keepdims=True)
        hit = lane_f == first
        score = jnp.where(hit, REMOVED, score)
        picks.append(first)
    return picks


def _pair_heads(o_even, o_odd, kvh):
    lane = _lane(o_even.shape)
    a = o_even if kvh == 0 else _swap64(o_even)
    b = _swap64(o_odd) if kvh == 0 else o_odd
    return jnp.where(lane < 64, a, b)


def _gate_col(gates, hh, c):
    col = 32 + hh * 3 + c
    return gates[:, col:col + 1]


def _cmp_prompt_kernel(q_ref, bc_ref, g_ref, o_ref, sel_ref, *, tq, cmp_block, n_sel):
    i = pl.program_id(1)
    nc = bc_ref.shape[0]
    gates = g_ref[...]
    bc = bc_ref[...]
    kc = bc[:, 0:LANES].astype(BF16)
    vc = bc[:, LANES:2 * LANES].astype(BF16)
    pos = i * tq + lax.broadcasted_iota(jnp.int32, (tq, 1), 0)
    lane = _lane((tq, nc))
    valid = lane * cmp_block + (cmp_block - 1) <= pos
    cmp_per_sel = SEL_BLOCK // cmp_block
    n_q = q_ref.shape[0]
    grp = n_q // 2
    for kvh in range(2):
        imp = jnp.zeros((tq, nc), F32)
        outs = []
        for g in range(grp):
            hh = kvh * grp + g
            p, o = _cmp_softmax(q_ref[hh], kc, vc, valid)
            imp = imp + p
            outs.append(o * _gate_col(gates, hh, 0))
        for g2 in range(grp // 2):
            blk = kvh * (grp // 2) + g2
            o_ref[:, blk * LANES:(blk + 1) * LANES] = _pair_heads(outs[2 * g2], outs[2 * g2 + 1], kvh)
        picks = _select(imp, pos, n_sel, cmp_per_sel, cmp_block)
        lane_f = lane.astype(F32)
        selm = jnp.zeros((tq, nc), F32)
        for first in picks:
            selm = jnp.where(lane_f == first, 1.0, selm)
        sel_ref[kvh] = selm.astype(BF16)


def _cmp_prompt(qraw, blocks_c, gates, *, b, t, tq, cmp_block):
    n_q = qraw.shape[0]
    nq = t // tq
    nc = t // cmp_block
    n_sel = min(N_SEL, t // SEL_BLOCK)
    return pl.pallas_call(
        functools.partial(_cmp_prompt_kernel, tq=tq, cmp_block=cmp_block, n_sel=n_sel), name="cmp_prompt",
        grid=(b, nq),
        in_specs=[
            pl.BlockSpec((n_q, tq, LANES), lambda bb, i: (0, bb * nq + i, 0)),
            pl.BlockSpec((nc, blocks_c.shape[1]), lambda bb, i: (bb, 0)),
            pl.BlockSpec((tq, LANES), lambda bb, i: (bb * nq + i, 0)),
        ],
        out_specs=[
            pl.BlockSpec((tq, n_q * 64), lambda bb, i: (bb * nq + i, 0)),
            pl.BlockSpec((2, tq, nc), lambda bb, i: (0, bb * nq + i, 0)),
        ],
        out_shape=[jax.ShapeDtypeStruct((b * t, n_q * 64), F32),
                   jax.ShapeDtypeStruct((2, b * t, nc), BF16)],
        compiler_params=_cparams(("arbitrary", "arbitrary")),
    )(qraw, blocks_c, gates)


def _sel_prompt_kernel(q_ref, k_ref, v_ref, sel_ref, g_ref, o_ref, m_scr, l_scr, acc_scr, *, tq, tk, cmp_block):
    kvh, i, j = pl.program_id(1), pl.program_id(2), pl.program_id(3)
    grp = q_ref.shape[0]
    nc = sel_ref.shape[2]
    jmax = (i * tq + tq - 1) // tk
    cmp_per_sel = SEL_BLOCK // cmp_block

    @pl.when(j == 0)
    def _():
        m_scr[...] = jnp.full(m_scr.shape, NEG_INF, F32)
        l_scr[...] = jnp.zeros(l_scr.shape, F32)
        acc_scr[...] = jnp.zeros(acc_scr.shape, F32)

    @pl.when(j <= jmax)
    def _():
        q = q_ref[...].reshape(grp * tq, LANES)
        s = _nt(q, k_ref[...]).reshape(grp, tq, tk)
        row_l = lax.broadcasted_iota(jnp.int32, (nc, tk), 0)
        key = j * tk + lax.broadcasted_iota(jnp.int32, (nc, tk), 1)
        expand = (row_l == (key // SEL_BLOCK) * cmp_per_sel).astype(BF16)
        msel = _nn(sel_ref[0], expand)
        qpos = i * tq + lax.broadcasted_iota(jnp.int32, (tq, tk), 0)
        kpos = j * tk + lax.broadcasted_iota(jnp.int32, (tq, tk), 1)
        valid = ((msel > 0.5) & (kpos <= qpos))[None]
        s = jnp.where(valid, s, NEG_INF)
        m_old = m_scr[...].reshape(grp, tq, 1)
        m_new = jnp.maximum(m_old, jnp.max(s, axis=-1, keepdims=True))
        alpha = jnp.exp(m_old - m_new)
        p = jnp.where(valid, jnp.exp(s - m_new), 0.0)
        l_scr[...] = (l_scr[...].reshape(grp, tq, 1) * alpha + jnp.sum(p, axis=-1, keepdims=True)).reshape(grp * tq, 1)
        pv = _nn(p.reshape(grp * tq, tk).astype(BF16), v_ref[...])
        acc_scr[...] = acc_scr[...] * alpha.reshape(grp * tq, 1) + pv
        m_scr[...] = m_new.reshape(grp * tq, 1)

    @pl.when(j == pl.num_programs(3) - 1)
    def _():
        l = l_scr[...]
        o = (acc_scr[...] / jnp.where(l == 0.0, 1.0, l)).reshape(grp, tq, LANES)
        gates = g_ref[...]
        for g2 in range(grp // 2):
            for kk in range(2):
                @pl.when(kvh == kk)
                def _():
                    a = o[2 * g2] * _gate_col(gates, kk * grp + 2 * g2, 1)
                    b2 = o[2 * g2 + 1] * _gate_col(gates, kk * grp + 2 * g2 + 1, 1)
                    o_ref[:, g2 * LANES:(g2 + 1) * LANES] = _pair_heads(a, b2, kk)


def _sel_prompt(qrope, kvs, selmask, gates, *, b, t, tq, tk, cmp_block):
    n_q = qrope.shape[0]
    grp = n_q // 2
    nq, nk = t // tq, t // tk
    nc = selmask.shape[2]

    def kv_map(lane_blk):
        def f(bb, kvh, i, j):
            return (bb * nk + jnp.minimum(j, (i * tq + tq - 1) // tk), lane_blk)
        return f

    return pl.pallas_call(
        functools.partial(_sel_prompt_kernel, tq=tq, tk=tk, cmp_block=cmp_block), name="sel_prompt",
        grid=(b, 2, nq, nk),
        in_specs=[
            pl.BlockSpec((grp, tq, LANES), lambda bb, kvh, i, j: (kvh, bb * nq + i, 0)),
            pl.BlockSpec((tk, LANES), kv_map(0)),
            pl.BlockSpec((tk, LANES), kv_map(1)),
            pl.BlockSpec((1, tq, nc), lambda bb, kvh, i, j: (kvh, bb * nq + i, 0)),
            pl.BlockSpec((tq, LANES), lambda bb, kvh, i, j: (bb * nq + i, 0)),
        ],
        out_specs=pl.BlockSpec((tq, grp * 64), lambda bb, kvh, i, j: (bb * nq + i, kvh)),
        out_shape=jax.ShapeDtypeStruct((b * t, n_q * 64), F32),
        scratch_shapes=[pltpu.VMEM((grp * tq, 1), F32), pltpu.VMEM((grp * tq, 1), F32),
                        pltpu.VMEM((grp * tq, LANES), F32)],
        compiler_params=_cparams(("arbitrary",) * 4),
    )(qrope, kvs, kvs, selmask, gates)


def _win_prompt_kernel(q_ref, k0, k1, k2, v0, v1, v2, g_ref, o_ref, *, tq, window):
    kvh, i = pl.program_id(1), pl.program_id(2)
    grp = q_ref.shape[0]
    q = q_ref[...].reshape(grp * tq, LANES)
    qpos = i * tq + lax.broadcasted_iota(jnp.int32, (tq, tq), 0)
    col = lax.broadcasted_iota(jnp.int32, (tq, tq), 1)
    ss, vs = [], []
    for d, (kr, vr) in enumerate(((k0, v0), (k1, v1), (k2, v2))):
        blk = i - 2 + d
        kpos = blk * tq + col
        valid = ((kpos >= 0) & (kpos <= qpos) & (kpos > qpos - window))[None]
        s = _nt(q, kr[...]).reshape(grp, tq, tq)
        ss.append(jnp.where(valid, s, NEG_INF))
        vs.append(valid)
    s = jnp.concatenate(ss, axis=2)
    valid = jnp.concatenate(vs, axis=2)
    m = jnp.max(s, axis=-1, keepdims=True)
    e = jnp.where(valid, jnp.exp(s - m), 0.0)
    l = jnp.sum(e, axis=-1, keepdims=True)
    p = (e / jnp.where(l == 0.0, 1.0, l)).reshape(grp * tq, 3 * tq).astype(BF16)
    o = _nn(p[:, 0:tq], v0[...]) + _nn(p[:, tq:2 * tq], v1[...]) + _nn(p[:, 2 * tq:3 * tq], v2[...])
    o = o.reshape(grp, tq, LANES)
    gates = g_ref[...]
    for g2 in range(grp // 2):
        for kk in range(2):
            @pl.when(kvh == kk)
            def _():
                a = o[2 * g2] * _gate_col(gates, kk * grp + 2 * g2, 2)
                b2 = o[2 * g2 + 1] * _gate_col(gates, kk * grp + 2 * g2 + 1, 2)
                o_ref[:, g2 * LANES:(g2 + 1) * LANES] = _pair_heads(a, b2, kk)


def _win_prompt(qrope, kvw, gates, *, b, t, tq, window):
    assert window == 2 * tq
    n_q = qrope.shape[0]
    grp = n_q // 2
    nq = t // tq

    def kv_map(d, lane_blk):
        def f(bb, kvh, i):
            return (bb * nq + jnp.maximum(i - 2 + d, 0), lane_blk)
        return f

    kv_specs = [pl.BlockSpec((tq, LANES), kv_map(d, lb)) for lb in (0, 1) for d in range(3)]
    return pl.pallas_call(
        functools.partial(_win_prompt_kernel, tq=tq, window=window), name="win_prompt",
        grid=(b, 2, nq),
        in_specs=[pl.BlockSpec((grp, tq, LANES), lambda bb, kvh, i: (kvh, bb * nq + i, 0))] + kv_specs
                 + [pl.BlockSpec((tq, LANES), lambda bb, kvh, i: (bb * nq + i, 0))],
        out_specs=pl.BlockSpec((tq, grp * 64), lambda bb, kvh, i: (bb * nq + i, kvh)),
        out_shape=jax.ShapeDtypeStruct((b * t, n_q * 64), F32),
        compiler_params=_cparams(("arbitrary",) * 3),
    )(qrope, kvw, kvw, kvw, kvw, kvw, kvw, gates)


def _mla_sample_kernel(pt_ref, q_ref, knew_ref, wuv_ref, cache_ref, o_ref, buf, sem, *,
                       pages, n_chunks, page_rows, ts, lat):
    b = pl.program_id(0)
    nb = pl.num_programs(0)
    rows = q_ref.shape[1]

    def copies(step, sl):
        return [pltpu.make_async_copy(cache_ref.at[pt_ref[step * pages + p]], buf.at[sl, p], sem.at[sl])
                for p in range(pages)]

    @pl.when(b == 0)
    def _():
        for c in copies(0, 0):
            c.start()

    q = q_ref[0]
    q_lat, q_rope = q[:, 0:LANES], q[:, LANES:lat]

    def chunk(c, carry):
        m_old, l_old, acc = carry
        step = b * n_chunks + c
        slot = step % 2

        @pl.when(step + 1 < nb * n_chunks)
        def _():
            for cp in copies(step + 1, 1 - slot):
                cp.start()

        for cp in copies(step, slot):
            cp.wait()
        kb = [buf[slot, pg].astype(BF16) for pg in range(pages)]
        s = jnp.concatenate([_nn(q_lat, k[0:LANES, :]) + _nn(q_rope, k[LANES:lat, :]) for k in kb], axis=1)
        m_new = jnp.maximum(m_old, jnp.max(s, axis=-1, keepdims=True))
        alpha = jnp.exp(m_old - m_new)
        p = jnp.exp(s - m_new)
        l_new = l_old * alpha + jnp.sum(p, axis=-1, keepdims=True)
        pb = p.astype(BF16)
        acc = acc * alpha
        for pg, k in enumerate(kb):
            acc = acc + _nt(pb[:, pg * page_rows:(pg + 1) * page_rows], k[0:LANES, :])
        return m_new, l_new, acc

    init = (jnp.full((rows, 1), NEG_INF, F32), jnp.zeros((rows, 1), F32), jnp.zeros((rows, LANES), F32))
    m_old, l_old, acc = lax.fori_loop(0, n_chunks, chunk, init)

    kn = knew_ref[0]
    s = _nt(q, kn)
    tok = lax.broadcasted_iota(jnp.int32, s.shape, 0) % ts
    col = lax.broadcasted_iota(jnp.int32, s.shape, 1)
    valid = (col <= tok) & (col < ts)
    s = jnp.where(valid, s, NEG_INF)
    m_new = jnp.maximum(m_old, jnp.max(s, axis=-1, keepdims=True))
    alpha = jnp.exp(m_old - m_new)
    p = jnp.where(valid, jnp.exp(s - m_new), 0.0)
    l_new = l_old * alpha + jnp.sum(p, axis=-1, keepdims=True)
    acc = acc * alpha + _nn(p.astype(BF16), kn[:, 0:LANES])
    o = (acc / l_new).astype(BF16)
    full = _nn(o, wuv_ref[...])
    nh = rows // ts
    vw = wuv_ref.shape[1] // nh
    lane_head = _lane((ts, full.shape[1])) // vw
    out = jnp.zeros((ts, full.shape[1]), F32)
    for h in range(nh):
        out = jnp.where(lane_head == h, full[h * ts:(h + 1) * ts, :], out)
    o_ref[0] = out


def _mla_sample(page_table, q_s, knew, wuv_flat, cache, *, pages, ts):
    bs, n_pages = page_table.shape
    lat, page_rows = cache.shape[1], cache.shape[2]
    n_chunks = n_pages // pages
    rows = q_s.shape[1]
    width = q_s.shape[2]
    grid_spec = pltpu.PrefetchScalarGridSpec(
        num_scalar_prefetch=1, grid=(bs,),
        in_specs=[
            pl.BlockSpec((1, rows, width), lambda b, pt: (b, 0, 0)),
            pl.BlockSpec((1,) + knew.shape[1:], lambda b, pt: (b, 0, 0)),
            pl.BlockSpec(wuv_flat.shape, lambda b, pt: (0, 0)),
            pl.BlockSpec(memory_space=pl.ANY),
        ],
        out_specs=pl.BlockSpec((1, ts, wuv_flat.shape[1]), lambda b, pt: (b, 0, 0)),
        scratch_shapes=[pltpu.VMEM((2, pages, lat, page_rows), F32), pltpu.SemaphoreType.DMA((2,))],
    )
    return pl.pallas_call(
        functools.partial(_mla_sample_kernel, pages=pages, n_chunks=n_chunks, page_rows=page_rows, ts=ts, lat=lat),
        name="mla_sample",
        grid_spec=grid_spec,
        out_shape=jax.ShapeDtypeStruct((bs, ts, wuv_flat.shape[1]), F32),
        compiler_params=_cparams(("arbitrary",)),
    )(page_table.reshape(-1), q_s, knew, wuv_flat, cache)


def _cmp_sample_kernel(q_ref, bc_ref, g_ref, o_ref, idx_ref, *, ts, past, cmp_block, n_sel, grp):
    nb = q_ref.shape[0]
    nc = bc_ref.shape[1]
    rows = ts * grp
    cmp_per_sel = SEL_BLOCK // cmp_block
    lane = _lane((rows, nc))
    pos_r = past + lax.broadcasted_iota(jnp.int32, (rows, 1), 0) // grp
    valid = lane * cmp_block + (cmp_block - 1) <= pos_r
    imps = []
    for bb in range(nb):
        bc = bc_ref[bb]
        kc = bc[:, 0:LANES].astype(BF16)
        vc = bc[:, LANES:2 * LANES].astype(BF16)
        q = q_ref[bb]
        for kvh in range(2):
            p, o = _cmp_softmax(q[kvh * rows:(kvh + 1) * rows], kc, vc, valid)
            imps.append(jnp.sum(p.reshape(ts, grp, nc), axis=1))
            o_k = o if kvh == 0 else _swap64(o)
            o_ref[bb, kvh * rows:(kvh + 1) * rows, :] = o_k[:, 0:64] * g_ref[bb, kvh * rows:(kvh + 1) * rows, 0:1]
    imp = jnp.concatenate(imps, axis=0)
    n_rows = imp.shape[0]
    pos_t = past + lax.broadcasted_iota(jnp.int32, (n_rows, 1), 0) % ts
    picks = _select(imp, pos_t, n_sel, cmp_per_sel, cmp_block)
    lane_o = _lane((n_rows, LANES))
    idx = jnp.zeros((n_rows, LANES), jnp.int32)
    for it, first in enumerate(picks):
        idx = jnp.where(lane_o == it, first.astype(jnp.int32) // cmp_per_sel, idx)
    idx_ref[...] = idx.reshape(nb, 2 * ts, LANES)


def _cmp_sample(q_s, blocks_c, grow, *, ts, past, cmp_block, n_sel, grp, nb):
    bs, qrows = q_s.shape[0], q_s.shape[1]
    return pl.pallas_call(
        functools.partial(_cmp_sample_kernel, ts=ts, past=past, cmp_block=cmp_block, n_sel=n_sel, grp=grp),
        name="cmp_sample",
        grid=(bs // nb,),
        in_specs=[
            pl.BlockSpec((nb,) + q_s.shape[1:], lambda b: (b, 0, 0)),
            pl.BlockSpec((nb,) + blocks_c.shape[1:], lambda b: (b, 0, 0)),
            pl.BlockSpec((nb,) + grow.shape[1:], lambda b: (b, 0, 0)),
        ],
        out_specs=[
            pl.BlockSpec((nb, qrows, 64), lambda b: (b, 0, 0)),
            pl.BlockSpec((nb, 2 * ts, LANES), lambda b: (b, 0, 0)),
        ],
        out_shape=[jax.ShapeDtypeStruct((bs, qrows, 64), F32),
                   jax.ShapeDtypeStruct((bs, 2 * ts, LANES), jnp.int32)],
        compiler_params=_cparams(("arbitrary",)),
    )(q_s, blocks_c, grow)


def _selwin_sample_kernel(idx_ref, pt_ref, q_ref, win_ref, wnew_ref, g_ref, cache_ref, snew_ref,
                          osel_ref, owin_ref, buf, sem, *,
                          ts, grp, n_sel, past_blocks, per_page, n_pages):
    b = pl.program_id(0)
    nb = pl.num_programs(0)
    slot = b % 2

    def issue(bb, sl, wait):
        for kt in range(2 * ts):
            kvh = kt // ts
            for it in range(n_sel):
                blk = idx_ref[(bb * 2 * ts + kt) * n_sel + it]
                for s in range(2):
                    rows_s = pl.ds(s * LANES + kvh * 64, 64)
                    dst = buf.at[sl, kt, s, :, pl.ds(it * LANES, LANES)]
                    if wait:
                        pltpu.make_async_copy(snew_ref.at[bb, rows_s, :], dst, sem.at[sl]).wait()
                        continue
                    in_past = blk < past_blocks

                    @pl.when(in_past)
                    def _():
                        phys = pt_ref[bb * n_pages + blk // per_page]
                        pltpu.make_async_copy(cache_ref.at[phys, rows_s, :], dst, sem.at[sl]).start()

                    @pl.when(jnp.logical_not(in_past))
                    def _():
                        pltpu.make_async_copy(snew_ref.at[bb, rows_s, :], dst, sem.at[sl]).start()

    @pl.when(b == 0)
    def _():
        issue(0, 0, False)

    @pl.when(b + 1 < nb)
    def _():
        issue(b + 1, 1 - slot, False)

    issue(b, slot, True)

    qf = q_ref[0].astype(F32)
    half_rows = qf.shape[0] // 2
    q = jnp.concatenate([qf[0:half_rows, 0:64], _swap64(qf[half_rows:, :])[:, 0:64]], axis=0).astype(BF16)
    gates = g_ref[0]
    nkeys = n_sel * LANES
    lane_k = _lane((1, nkeys))
    within = lane_k % LANES
    rows = ts * grp
    for kvh in range(2):
        for t in range(ts):
            kt = kvh * ts + t
            kk = buf[slot, kt, 0].astype(BF16)
            vv = buf[slot, kt, 1].astype(BF16)
            lo = jnp.zeros((1, nkeys), jnp.int32)
            hi = jnp.zeros((1, nkeys), jnp.int32)
            for it in range(n_sel):
                blk = idx_ref[(b * 2 * ts + kt) * n_sel + it]
                in_past = blk < past_blocks
                half = (blk % per_page) * SEL_BLOCK
                here = (lane_k // LANES) == it
                lo = jnp.where(here, jnp.where(in_past, half, 0), lo)
                hi = jnp.where(here, jnp.where(in_past, half + SEL_BLOCK - 1, t), hi)
            kvalid = (within >= lo) & (within <= hi)
            r0 = kt * grp
            s = jnp.where(kvalid, _nn(q[r0:r0 + grp], kk), NEG_INF)
            m = jnp.max(s, axis=-1, keepdims=True)
            e = jnp.where(kvalid, jnp.exp(s - m), 0.0)
            l = jnp.sum(e, axis=-1, keepdims=True)
            p = e / jnp.where(l == 0.0, 1.0, l)
            osel_ref[0, r0:r0 + grp, :] = _nt(p.astype(BF16), vv) * gates[r0:r0 + grp, 1:2]

        r1 = kvh * rows
        wk = win_ref[0, kvh * 64:(kvh + 1) * 64, :].astype(BF16)
        wv = win_ref[0, LANES + kvh * 64:LANES + (kvh + 1) * 64, :].astype(BF16)
        nk_ = wnew_ref[0, kvh * 64:(kvh + 1) * 64, :].astype(BF16)
        nv_ = wnew_ref[0, LANES + kvh * 64:LANES + (kvh + 1) * 64, :].astype(BF16)
        wlen = wk.shape[1]
        qk = q[r1:r1 + rows]
        tok = lax.broadcasted_iota(jnp.int32, (rows, 1), 0) // grp
        v_old = _lane((rows, wlen)) > tok
        col_n = _lane((rows, LANES))
        v_new = (col_n <= tok) & (col_n < ts)
        s_old = jnp.where(v_old, _nn(qk, wk), NEG_INF)
        s_new = jnp.where(v_new, _nn(qk, nk_), NEG_INF)
        m = jnp.maximum(jnp.max(s_old, axis=-1, keepdims=True), jnp.max(s_new, axis=-1, keepdims=True))
        e_old = jnp.where(v_old, jnp.exp(s_old - m), 0.0)
        e_new = jnp.where(v_new, jnp.exp(s_new - m), 0.0)
        l = jnp.sum(e_old, axis=-1, keepdims=True) + jnp.sum(e_new, axis=-1, keepdims=True)
        inv = 1.0 / jnp.where(l == 0.0, 1.0, l)
        ow = _nt((e_old * inv).astype(BF16), wv) + _nt((e_new * inv).astype(BF16), nv_)
        owin_ref[0, r1:r1 + rows, :] = ow * gates[r1:r1 + rows, 2:3]


def _selwin_sample(idx_flat, page_table, q_c, win_t, wnew_t, grow, cache_sel_t, snew_t, *,
                   ts, grp, n_sel, past_blocks, per_page):
    bs, n_pages = page_table.shape
    qrows = q_c.shape[1]
    grid_spec = pltpu.PrefetchScalarGridSpec(
        num_scalar_prefetch=2, grid=(bs,),
        in_specs=[
            pl.BlockSpec((1,) + q_c.shape[1:], lambda b, i, p: (b, 0, 0)),
            pl.BlockSpec((1,) + win_t.shape[1:], lambda b, i, p: (b, 0, 0)),
            pl.BlockSpec((1,) + wnew_t.shape[1:], lambda b, i, p: (b, 0, 0)),
            pl.BlockSpec((1,) + grow.shape[1:], lambda b, i, p: (b, 0, 0)),
            pl.BlockSpec(memory_space=pl.ANY),
            pl.BlockSpec(memory_space=pl.ANY),
        ],
        out_specs=[pl.BlockSpec((1, qrows, 64), lambda b, i, p: (b, 0, 0)),
                   pl.BlockSpec((1, qrows, 64), lambda b, i, p: (b, 0, 0))],
        scratch_shapes=[pltpu.VMEM((2, 2 * ts, 2, 64, n_sel * LANES), F32), pltpu.SemaphoreType.DMA((2,))],
    )
    return pl.pallas_call(
        functools.partial(_selwin_sample_kernel, ts=ts, grp=grp, n_sel=n_sel, past_blocks=past_blocks,
                          per_page=per_page, n_pages=n_pages),
        name="selwin_sample", grid_spec=grid_spec,
        out_shape=[jax.ShapeDtypeStruct((bs, qrows, 64), F32)] * 2,
        compiler_params=_cparams(("arbitrary",)),
    )(idx_flat, page_table.reshape(-1), q_c, win_t, wnew_t, grow, cache_sel_t, snew_t)


def _outproj_kernel(h_ref, oa_ref, oc_ref, os_ref, ow_ref, wo_ref, gf_ref, wr_ref, br_ref,
                    h1_ref, xn_ref, e_ref, gw_ref, *, top_k):
    half = oa_ref.shape[1]
    ob = (oc_ref[...] + os_ref[...] + ow_ref[...]).astype(BF16)
    h1 = h_ref[...] + _nn(oa_ref[...], wo_ref[0:half, :]) + _nn(ob, wo_ref[half:2 * half, :])
    h1_ref[...] = h1
    xn = h1 * lax.rsqrt(jnp.mean(h1 * h1, axis=-1, keepdims=True) + RMS_EPS) * gf_ref[...]
    xn_ref[...] = xn
    hi, lo = _split(xn)
    w_hi, w_lo = _split(wr_ref[...])
    logits = _nn(hi, w_hi) + _nn(lo, w_hi) + _nn(hi, w_lo) + br_ref[...]
    lane = _lane(logits.shape)
    lane_f = lane.astype(F32)
    score = logits
    tops, es = [], []
    for _ in range(top_k):
        m = jnp.max(score, axis=-1, keepdims=True)
        first = jnp.min(jnp.where(score == m, lane_f, 1e9), axis=-1, keepdims=True)
        score = jnp.where(lane_f == first, REMOVED, score)
        tops.append(m)
        es.append(first)
    ws = [jnp.exp(t - tops[0]) for t in tops]
    tot = functools.reduce(lambda a, b2: a + b2, ws)
    e_out = jnp.zeros(logits.shape, jnp.int32)
    w_out = jnp.zeros(logits.shape, F32)
    for k in range(top_k):
        e_out = jnp.where(lane == k, es[k].astype(jnp.int32), e_out)
        w_out = jnp.where(lane == k, ws[k] / tot, w_out)
    e_ref[...] = e_out
    gw_ref[...] = w_out


def _outproj(h, oa, oc, os_, ow, wo, gf, wrt, br, *, tm, top_k):
    n, d = h.shape
    row = lambda i: (i, 0)
    full = lambda i: (0, 0)
    return pl.pallas_call(
        functools.partial(_outproj_kernel, top_k=top_k), name="outproj",
        grid=(n // tm,),
        in_specs=[pl.BlockSpec((tm, d), row), pl.BlockSpec((tm, oa.shape[1]), row),
                  pl.BlockSpec((tm, oc.shape[1]), row), pl.BlockSpec((tm, oc.shape[1]), row),
                  pl.BlockSpec((tm, oc.shape[1]), row),
                  pl.BlockSpec(wo.shape, full), pl.BlockSpec(gf.shape, full),
                  pl.BlockSpec(wrt.shape, full), pl.BlockSpec(br.shape, full)],
        out_specs=[pl.BlockSpec((tm, d), row), pl.BlockSpec((tm, d), row),
                   pl.BlockSpec((tm, LANES), row), pl.BlockSpec((tm, LANES), row)],
        out_shape=[jax.ShapeDtypeStruct((n, d), F32), jax.ShapeDtypeStruct((n, d), F32),
                   jax.ShapeDtypeStruct((n, LANES), jnp.int32), jax.ShapeDtypeStruct((n, LANES), F32)],
        compiler_params=_cparams(("parallel",)),
    )(h, oa, oc, os_, ow, wo, gf, wrt, br)


def _gather_kernel(idx_ref, src_ref, o_ref, sem, *, rows):
    g = pl.program_id(0)

    def start(r2, c):
        for pr in range(2):
            r = 2 * r2 + pr
            pltpu.make_async_copy(src_ref.at[pl.ds(idx_ref[g * rows + r], 1)], o_ref.at[pl.ds(r, 1)],
                                  sem.at[0]).start(priority=pr)
        return c

    lax.fori_loop(0, rows // 2, start, 0, unroll=4)
    pltpu.make_async_copy(src_ref.at[pl.ds(0, rows)], o_ref, sem.at[0]).wait()


def _gather_rows(idx, src, *, rows):
    n = idx.shape[0]
    d = src.shape[1]
    grid_spec = pltpu.PrefetchScalarGridSpec(
        num_scalar_prefetch=1, grid=(n // rows,),
        in_specs=[pl.BlockSpec(memory_space=pl.ANY)],
        out_specs=pl.BlockSpec((rows, d), lambda g, idx_: (g, 0)),
        scratch_shapes=[pltpu.SemaphoreType.DMA((1,))],
    )
    return pl.pallas_call(
        functools.partial(_gather_kernel, rows=rows), name="gather_rows",
        grid_spec=grid_spec,
        out_shape=jax.ShapeDtypeStruct((n, d), src.dtype),
        compiler_params=_cparams(("arbitrary",)),
    )(idx, src)


def _expert_kernel(be_ref, nu_ref, x_ref, w1_ref, b1_ref, w2_ref, b2_ref, o_ref, w1s, w2s, *, d_ff):
    i = pl.program_id(0)
    prev = be_ref[jnp.maximum(i - 1, 0)]

    @pl.when((i == 0) | (be_ref[i] != prev))
    def _():
        w1s[...] = w1_ref[0].astype(BF16)
        w2s[...] = w2_ref[0].astype(BF16)

    @pl.when(i < nu_ref[0])
    def _():
        hb = _nn(x_ref[...].astype(BF16), w1s[...]) + b1_ref[0]
        g = jnp.minimum(hb[:, 0:d_ff], SWIGLU_LIMIT)
        u = jnp.clip(hb[:, d_ff:2 * d_ff], -SWIGLU_LIMIT, SWIGLU_LIMIT)
        act = (u + 1.0) * g * jax.nn.sigmoid(SWIGLU_ALPHA * g)
        o_ref[...] = _nn(act.astype(BF16), w2s[...]) + b2_ref[0]

    @pl.when(i >= nu_ref[0])
    def _():
        o_ref[...] = jnp.zeros(o_ref.shape, F32)


def _experts(block_e, n_used, x_sorted, w1, b1, w2, b2, *, bm):
    r, d = x_sorted.shape
    d_ff = w2.shape[1]
    grid_spec = pltpu.PrefetchScalarGridSpec(
        num_scalar_prefetch=2, grid=(r // bm,),
        in_specs=[
            pl.BlockSpec((bm, d), lambda i, be, nu: (i, 0)),
            pl.BlockSpec((1,) + w1.shape[1:], lambda i, be, nu: (be[i], 0, 0)),
            pl.BlockSpec((1, 1, b1.shape[2]), lambda i, be, nu: (be[i], 0, 0)),
            pl.BlockSpec((1,) + w2.shape[1:], lambda i, be, nu: (be[i], 0, 0)),
            pl.BlockSpec((1, 1, b2.shape[2]), lambda i, be, nu: (be[i], 0, 0)),
        ],
        out_specs=pl.BlockSpec((bm, d), lambda i, be, nu: (i, 0)),
        scratch_shapes=[pltpu.VMEM(w1.shape[1:], BF16), pltpu.VMEM(w2.shape[1:], BF16)],
    )
    return pl.pallas_call(
        functools.partial(_expert_kernel, d_ff=d_ff), name="experts",
        grid_spec=grid_spec,
        out_shape=jax.ShapeDtypeStruct((r, d), F32),
        compiler_params=_cparams(("arbitrary",)),
    )(block_e, n_used, x_sorted, w1, b1, w2, b2)


def _final_kernel(h1_ref, y_ref, gw_ref, p_ref, wp_ref, gp_ref, wg_ref, gfin_ref, o_ref, *, top_k):
    h2 = h1_ref[...]
    gw = gw_ref[...]
    for k in range(top_k):
        h2 = h2 + y_ref[k] * gw[:, k:k + 1]
    hn = h2 * lax.rsqrt(jnp.mean(h2 * h2, axis=-1, keepdims=True) + RMS_EPS) * gp_ref[...]
    gate = jax.nn.sigmoid(_nn(hn.astype(BF16), wg_ref[...]))
    h3 = h2 + _nn(p_ref[...].astype(BF16), wp_ref[...]) * gate
    o_ref[...] = h3 * lax.rsqrt(jnp.mean(h3 * h3, axis=-1, keepdims=True) + RMS_EPS) * gfin_ref[...]


def _final(h1, y4, gate_w, p, wp, gp, wg, gfin, *, tm, top_k):
    n, d = h1.shape
    row = lambda i: (i, 0)
    full = lambda i: (0, 0)
    return pl.pallas_call(
        functools.partial(_final_kernel, top_k=top_k), name="final",
        grid=(n // tm,),
        in_specs=[pl.BlockSpec((tm, d), row), pl.BlockSpec((top_k, tm, d), lambda i: (0, i, 0)),
                  pl.BlockSpec((tm, LANES), row), pl.BlockSpec((tm, p.shape[1]), row),
                  pl.BlockSpec(wp.shape, full), pl.BlockSpec(gp.shape, full), pl.BlockSpec(wg.shape, full),
                  pl.BlockSpec(gfin.shape, full)],
        out_specs=pl.BlockSpec((tm, d), row),
        out_shape=jax.ShapeDtypeStruct((n, d), F32),
        compiler_params=_cparams(("parallel",)),
    )(h1, y4, gate_w, p, wp, gp, wg, gfin)


def _rope_tables(pos, dim, reps, width):
    inv = ROPE_THETA ** (-jnp.arange(0, dim, 2, dtype=F32) / dim)
    ang = pos.astype(F32)[:, None] * inv[None, :]
    cos, sin = jnp.cos(ang), jnp.sin(ang)
    c = jnp.tile(jnp.concatenate([cos, cos], axis=1), (1, reps))
    s = jnp.tile(jnp.concatenate([-sin, sin], axis=1), (1, reps))
    pad = width - c.shape[1]
    return jnp.pad(c, ((0, 0), (0, pad))), jnp.pad(s, ((0, 0), (0, pad)))


def _tile_rows(n, pref):
    for t in (pref, 256, 128, 64, 32, 16, 8):
        if t <= pref and n % t == 0:
            return t
    return n


def kernel(x_prompt, x_sample, cache_mla, cache_nsa_cmp, cache_nsa_sel, state_nsa_win, page_table, p_prompt, p_sample,
           g_attn, w_in, g_q, w_uq, g_kv, w_uk, w_uv, cmp_pe, w_cmp1, w_cmp2, w_o, g_ffn, w_router, b_router,
           w_e_in, b_e_in, w_e_out, b_e_out, g_ple, w_ple, w_ple_gate, g_final):
    depth = g_attn.shape[0]
    assert depth == 1
    b, t, d = x_prompt.shape
    bs, ts, _ = x_sample.shape
    n_pages = page_table.shape[1]
    page_rows = cache_mla.shape[2]
    past = n_pages * page_rows
    lat = cache_mla.shape[3]
    q_rank, n_heads, qk_dim = w_uq.shape[1:]
    kv_rank, _, nope = w_uk.shape[1:]
    rope_a = qk_dim - nope
    v_dim = w_uv.shape[3]
    cmp_block = cmp_pe.shape[1]
    kvh, hd = cache_nsa_cmp.shape[4], cache_nsa_cmp.shape[5]
    kv_cols = 2 * kvh * hd
    n_q = 8
    grp = n_q // kvh
    window = 512
    n_exp = w_router.shape[2]
    d_ff = w_e_out.shape[2]
    assert (kvh, hd, rope_a, kv_rank, q_rank, kv_cols) == (2, 64, 32, 128, 256, 256)
    np_, ns_ = b * t, bs * ts
    n = np_ + ns_
    mla_scale = float(qk_dim) ** -0.5
    nsa_scale = float(hd) ** -0.5

    wi = w_in[0]
    o_cq, o_ckv, o_kr, o_qb, o_cmp, o_sel, o_win, o_gate = 0, 256, 384, 416, 928, 1184, 1440, 1696
    misc_w = jnp.concatenate([wi[:, o_kr:o_kr + 32], wi[:, o_gate:o_gate + 24], jnp.zeros((d, 72), F32)], axis=1)
    win_r = jnp.concatenate([wi[:, o_cq:o_cq + 256], wi[:, o_ckv:o_ckv + 128], wi[:, o_qb:o_qb + 512],
                             wi[:, o_cmp:o_cmp + 256], wi[:, o_sel:o_sel + 256], wi[:, o_win:o_win + 256],
                             misc_w], axis=1).astype(BF16)
    wnope = jnp.transpose(w_uq[0][:, :, :nope], (1, 0, 2))
    wuk = jnp.transpose(w_uk[0], (1, 0, 2))
    wrope = jnp.transpose(w_uq[0][:, :, nope:], (1, 0, 2))
    wrope_p = jnp.concatenate([wrope[:, :, rope_a // 2:], wrope[:, :, :rope_a // 2]], axis=2)
    pad_r = lambda w: jnp.transpose(jnp.pad(w, ((0, 0), (0, 0), (0, LANES - rope_a))), (1, 0, 2)).reshape(
        q_rank, n_heads * LANES).astype(BF16)
    wr, wrp = pad_r(wrope), pad_r(wrope_p)
    wuv_t = jnp.transpose(w_uv[0], (1, 0, 2))
    zv = jnp.zeros_like(wuv_t)
    odd = (jnp.arange(n_heads) % 2 == 1)[:, None, None]
    wuv_h = jnp.concatenate([jnp.where(odd, zv, wuv_t), jnp.where(odd, wuv_t, zv)], axis=2).astype(BF16)
    wuv_flat = w_uv[0].reshape(kv_rank, n_heads * v_dim).astype(BF16)

    pe = cmp_pe[0]
    pe_full = jnp.broadcast_to(pe[:, :, None, :], (cmp_block, 2, kvh, hd)).reshape(cmp_block, kv_cols)
    w1 = w_cmp1[0].reshape(2, cmp_block, hd, -1)
    hid = w1.shape[3]
    eye = jnp.eye(kvh, dtype=F32)
    w1bd = jnp.einsum('srdh,kK->srkdKh', w1, eye).reshape(2, cmp_block, kvh * hd, kvh * hid).astype(BF16)
    w2bd = jnp.einsum('shd,kK->skhKd', w_cmp2[0], eye).reshape(2, kvh * hid, kvh * hd).astype(BF16)

    wo = w_o[0].astype(BF16)
    wrt = jnp.pad(w_router[0], ((0, 0), (0, LANES - n_exp)))
    br = jnp.concatenate([b_router[0], jnp.full((LANES - n_exp,), NEG_INF, F32)])[None, :]

    pos = jnp.concatenate([jnp.tile(jnp.arange(t, dtype=jnp.int32), b),
                           jnp.tile(past + jnp.arange(ts, dtype=jnp.int32), bs)])
    cosa, sina = _rope_tables(pos, rope_a, 1, LANES)
    cosb, sinb = _rope_tables(pos, hd, 2, LANES)

    x = jnp.concatenate([x_prompt.reshape(np_, d), x_sample.reshape(ns_, d)], axis=0)
    tm = _tile_rows(n, 256)
    (rows_a, kmla, qmla, rows_c, rows_s, rows_w, kvs, kvw, qraw, qrope, gates) = _proj(
        x, cosa, sina, cosb, sinb, g_attn, win_r, g_q, g_kv, wnope, wuk, wr, wrp,
        tm=tm, mla_scale=mla_scale, nsa_scale=nsa_scale)

    tq_a = _tile_rows(t, 256)
    tk_a = _tile_rows(t, 512)
    oa_p = _mla_prompt(qmla, kmla, wuv_h, b=b, t=t, tq=tq_a, tk=tk_a)

    pages_c = 32
    cpages = np_ // page_rows
    rows_c_pages = rows_c[:np_].reshape(cpages, page_rows, kv_cols)
    blocks_p = _compress(jnp.arange(cpages, dtype=jnp.int32), rows_c_pages, pe_full, w1bd, w2bd,
                         pages=min(pages_c, cpages), transposed=False)
    tq_b = _tile_rows(t, 256)
    oc_p, selmask = _cmp_prompt(qraw, blocks_p, gates, b=b, t=t, tq=tq_b, cmp_block=cmp_block)
    os_p = _sel_prompt(qrope, kvs, selmask, gates, b=b, t=t, tq=tq_b, tk=_tile_rows(t, 512), cmp_block=cmp_block)
    ow_p = _win_prompt(qrope, kvw, gates, b=b, t=t, tq=tq_b, window=window)

    q_s = jnp.transpose(qmla[:, np_:, :].reshape(n_heads, bs, ts, 256), (1, 0, 2, 3)).reshape(bs, n_heads * ts, 256)
    knew = jnp.pad(kmla[np_:].reshape(bs, ts, 256), ((0, 0), (0, 16 - ts), (0, 0)))
    rows_last = lambda c: jnp.transpose(c, (0, 2, 3, 4, 1)).reshape(c.shape[0], kv_cols, c.shape[1])
    cache_mla_t = jnp.transpose(cache_mla[0], (0, 2, 1))
    cache_cmp_t = rows_last(cache_nsa_cmp[0])
    cache_sel_t = rows_last(cache_nsa_sel[0])
    win_t = rows_last(state_nsa_win[0])
    pages_a = min(32, n_pages)
    oa_s = _mla_sample(page_table, q_s, knew, wuv_flat, cache_mla_t, pages=pages_a, ts=ts).reshape(ns_, -1)

    sum_past = _compress(page_table.reshape(-1), cache_cmp_t, pe_full, w1bd, w2bd,
                         pages=min(2 * pages_c, n_pages), transposed=True)
    per_page_c = page_rows // cmp_block
    new_len = -(-ts // SEL_BLOCK) * SEL_BLOCK
    new_pages = -(-new_len // page_rows)
    rc_new = jnp.pad(rows_c[np_:].reshape(bs, ts, kv_cols), ((0, 0), (0, new_pages * page_rows - ts), (0, 0)))
    sum_new = _compress(jnp.arange(bs * new_pages, dtype=jnp.int32), rc_new.reshape(bs * new_pages, page_rows, kv_cols),
                        pe_full, w1bd, w2bd, pages=min(pages_c, bs * new_pages), transposed=False)
    nc_s = past // cmp_block + new_len // cmp_block
    nc_pad = -(-nc_s // LANES) * LANES
    blocks_s = jnp.concatenate([
        sum_past.reshape(bs, n_pages * per_page_c, kv_cols),
        sum_new.reshape(bs, new_pages * per_page_c, kv_cols)[:, :new_len // cmp_block],
        jnp.zeros((bs, nc_pad - nc_s, kv_cols), F32)], axis=1)
    to_s = lambda q: jnp.transpose(q[:, np_:, :].reshape(kvh, grp, bs, ts, LANES), (2, 0, 3, 1, 4)).reshape(
        bs, kvh * ts * grp, LANES)
    grow = jnp.transpose(gates[np_:, 32:32 + 3 * n_q].reshape(bs, ts, kvh, grp, 3), (0, 2, 1, 3, 4)).reshape(
        bs, kvh * ts * grp, 3)
    grow = jnp.pad(grow, ((0, 0), (0, 0), (0, LANES - 3)))
    n_sel = min(N_SEL, nc_s * cmp_block // SEL_BLOCK)
    nb_c = 8 if bs % 8 == 0 else 1
    oc_s, idx = _cmp_sample(to_s(qraw), blocks_s, grow, ts=ts, past=past, cmp_block=cmp_block, n_sel=n_sel, grp=grp,
                            nb=nb_c)
    idx_flat = idx[:, :, :n_sel].reshape(-1)
    new_t = lambda r: jnp.pad(jnp.transpose(r[np_:].reshape(bs, ts, kv_cols), (0, 2, 1)),
                              ((0, 0), (0, 0), (0, page_rows - ts)))
    os_s, ow_s = _selwin_sample(idx_flat, page_table, to_s(qrope), win_t, new_t(rows_w), grow, cache_sel_t,
                                new_t(rows_s),
                                ts=ts, grp=grp, n_sel=n_sel, past_blocks=past // SEL_BLOCK,
                                per_page=page_rows // SEL_BLOCK)
    from_s = lambda o: jnp.transpose(o.reshape(bs, kvh, ts, grp, 64), (0, 2, 1, 3, 4)).reshape(ns_, n_q * 64)

    cat = lambda p_, s_: jnp.concatenate([p_, s_.reshape(ns_, -1).astype(p_.dtype)], axis=0)
    h1, xn, top_e, gate_w = _outproj(x, cat(oa_p, oa_s), cat(oc_p, from_s(oc_s)), cat(os_p, from_s(os_s)),
                                     cat(ow_p, from_s(ow_s)), wo, g_ffn, wrt, br, tm=tm, top_k=TOP_K)

    bm = 256
    flat_e = top_e[:, :TOP_K].reshape(-1)
    n_assign = flat_e.shape[0]
    onehot = (flat_e[:, None] == jnp.arange(n_exp, dtype=jnp.int32)[None, :]).astype(jnp.int32)
    rank = jnp.take_along_axis(jnp.cumsum(onehot, axis=0), flat_e[:, None], axis=1)[:, 0] - 1
    counts = jnp.sum(onehot, axis=0)
    padded = (counts + bm - 1) // bm * bm
    pad_end = jnp.cumsum(padded)
    dest = (pad_end - padded)[flat_e] + rank
    n_blocks = -(-n_assign // bm) + n_exp
    row_tok = jnp.zeros((n_blocks * bm,), jnp.int32).at[dest].set(jnp.arange(n_assign, dtype=jnp.int32) // TOP_K)
    block_start = jnp.arange(n_blocks, dtype=jnp.int32) * bm
    block_e = jnp.minimum(jnp.sum((pad_end[None, :] <= block_start[:, None]).astype(jnp.int32), axis=1), n_exp - 1)
    n_used = (pad_end[-1] // bm).astype(jnp.int32)[None]

    x_sorted = _gather_rows(row_tok, xn, rows=bm)
    y_sorted = _experts(block_e, n_used, x_sorted, w_e_in[0], b_e_in[0][:, None, :],
                        w_e_out[0], b_e_out[0][:, None, :], bm=bm)
    dest_k = jnp.transpose(dest.reshape(n, TOP_K)).reshape(-1).astype(jnp.int32)
    y4 = _gather_rows(dest_k, y_sorted, rows=_tile_rows(n, 256)).reshape(TOP_K, n, d)

    p_all = jnp.concatenate([p_prompt[0].reshape(np_, -1), p_sample[0].reshape(ns_, -1)], axis=0)
    y = _final(h1, y4, gate_w, p_all, w_ple[0].astype(BF16), g_ple, w_ple_gate[0].astype(BF16), g_final[None, :],
               tm=tm, top_k=TOP_K)

    kv_shape = (2, kvh, hd)
    wlen = state_nsa_win.shape[2]
    rows_w_p = rows_w[:np_].reshape(b, t, kv_cols)
    win_s = jnp.concatenate([state_nsa_win[0].reshape(bs, wlen, kv_cols), rows_w[np_:].reshape(bs, ts, kv_cols)], axis=1)[:, ts:]
    return (
        y[:np_].reshape(b, t, d), y[np_:].reshape(bs, ts, d),
        rows_a[:np_].reshape(1, b, t, lat), rows_a[np_:].reshape(1, bs, ts, lat),
        rows_c[:np_].reshape((1, b, t) + kv_shape), rows_c[np_:].reshape((1, bs, ts) + kv_shape),
        rows_s[:np_].reshape((1, b, t) + kv_shape), rows_s[np_:].reshape((1, bs, ts) + kv_shape),
        rows_w_p[:, t - min(window, t):].reshape((1, b, min(window, t)) + kv_shape),
        win_s.reshape((1, bs, wlen) + kv_shape),
    )
```
